```python
import math
import jax, jax.numpy as jnp
from jax import lax
import numpy as np

D_MODEL = 1024
BATCH = 8
SEQ = 8192
DEPTH = 2
DEC_BATCH = 32
DEC_SEQ = 2048
PAST_LEN = 128

GRID_W = 64
ROPE_THETA = 10000.0
Q_BLOCK = 128
RMS_EPS = 1e-6
LN_EPS = 1e-5

ALPHA = (2 * DEPTH) ** 0.25
BETA = (8 * DEPTH) ** -0.25

A_HEADS = 4
A_KDIM = 128
A_VDIM = 128
A_QF = A_HEADS * A_KDIM
A_WIDTH = A_HEADS * A_VDIM
A_CHUNK = 32
B_HEADS = 4
B_KV_HEADS = 2
B_GROUP = B_HEADS // B_KV_HEADS
B_HEAD_DIM = 128
B_WIDTH = B_HEADS * B_HEAD_DIM
B_KVW = B_KV_HEADS * B_HEAD_DIM
EVEN_IN = A_QF + 2 * A_QF + A_WIDTH + A_WIDTH + B_WIDTH + 2 * B_KVW
EVEN_MIX = A_WIDTH + B_WIDTH

C_HEADS = 8
C_NOPE = 128
C_ROPE = 64
C_V = 128
C_Q_LORA = 384
C_KV_LORA = 256
ODD_IN = C_Q_LORA + C_KV_LORA + C_ROPE
C_WIDTH = C_HEADS * C_V

N_EXPERTS = 256
TOP_K = 8
N_GROUPS = 8
TOPK_GROUPS = 4
D_EXPERT = 256
D_SHARED = 256
ROUTED_SCALE = 2.5
MOE_BLOCK = 128

N_EVEN = (DEPTH + 1) // 2
N_ODD = DEPTH // 2

kernel_name = "hybrid_hgrn2_gqa_mla_moe_encoder"

F32 = jnp.float32


def rms_norm(x, w):
    xf = x.astype(F32)
    y = xf * lax.rsqrt(jnp.mean(xf * xf, axis=-1, keepdims=True) + RMS_EPS) * w.astype(F32)
    return y.astype(x.dtype)


def layer_norm(x, g, b):
    xf = x.astype(F32)
    mu = jnp.mean(xf, axis=-1, keepdims=True)
    xc = xf - mu
    var = jnp.mean(xc * xc, axis=-1, keepdims=True)
    return (xc * lax.rsqrt(var + LN_EPS) * g.astype(F32) + b.astype(F32)).astype(x.dtype)


def split_cols(h, sizes):
    idx = np.cumsum(sizes)[:-1].tolist()
    return jnp.split(h, idx, axis=-1)


def axial_rope_tables(T, rot_dim):
    rows = T // GRID_W
    row = jnp.repeat(jnp.arange(rows), GRID_W).astype(F32)
    col = (jnp.arange(T) % GRID_W).astype(F32)
    half = rot_dim // 2
    inv = ROPE_THETA ** (-jnp.arange(0, half, 2, dtype=F32) / half)
    ang_r = row[:, None] * inv[None, :]
    ang_c = col[:, None] * inv[None, :]
    return (jnp.cos(ang_r), jnp.sin(ang_r), jnp.cos(ang_c), jnp.sin(ang_c))


def _rot_half(x, cos, sin):
    x1, x2 = jnp.split(x, 2, axis=-1)
    return jnp.concatenate([x1 * cos - x2 * sin, x1 * sin + x2 * cos], axis=-1)


def apply_axial_rope(x, tables):
    cr, sr, cc, sc = tables
    extra = x.ndim - 3
    expand = lambda t: t.reshape(t.shape[0], *([1] * extra), t.shape[1])
    xf = x.astype(F32)
    half = x.shape[-1] // 2
    out = jnp.concatenate([_rot_half(xf[..., :half], expand(cr), expand(sr)),
                           _rot_half(xf[..., half:], expand(cc), expand(sc))], axis=-1)
    return out.astype(x.dtype)


def chunk_gla(q, k, v, logf):
    Bsz, T, H, K = q.shape
    V = v.shape[-1]
    n = T // A_CHUNK
    to_chunks = lambda a: a.reshape(Bsz, n, A_CHUNK, H, a.shape[-1]).transpose(1, 0, 3, 2, 4)
    qc, kc, vc, gc = to_chunks(q), to_chunks(k), to_chunks(v), to_chunks(logf)
    causal_in_chunk = jnp.tril(jnp.ones((A_CHUNK, A_CHUNK), dtype=bool))[None, None, :, :, None]

    def step(S, inp):
        qb, kb, vb, gb = inp
        b = jnp.cumsum(gb, axis=2)
        rel = b[:, :, :, None, :] - b[:, :, None, :, :]
        decay = jnp.exp(jnp.where(causal_in_chunk, rel, -jnp.inf))
        attn = jnp.einsum('bhtk,bhsk,bhtsk->bhts', qb, kb, decay)
        o = jnp.einsum('bhts,bhsv->bhtv', attn, vb) + jnp.einsum('bhtk,bhkv->bhtv', qb * jnp.exp(b), S)
        b_last = b[:, :, -1, :]
        S = jnp.exp(b_last)[..., None] * S + jnp.einsum(
            'bhsk,bhsv->bhkv', kb * jnp.exp(b_last[:, :, None, :] - b), vb)
        return S, o

    S0 = jnp.zeros((Bsz, H, K, V), F32)
    _, o = lax.scan(step, S0, (qc, kc, vc, gc))
    return o.transpose(1, 0, 3, 2, 4).reshape(Bsz, T, H, V)


def hgrn2_mix(q, f_fwd, f_bwd, v, g, lb, norm_w):
    Bsz, T, _ = q.shape
    heads = lambda a, d: a.reshape(Bsz, T, A_HEADS, d)
    lb = lb.reshape(2, A_HEADS, A_KDIM)

    def gates(f, lbd):
        fg = lbd + (1.0 - lbd) * jax.nn.sigmoid(heads(f, A_KDIM).astype(F32))
        return jnp.log(fg), 1.0 - fg

    logf_f, k_f = gates(f_fwd, lb[0])
    logf_b, k_b = gates(f_bwd, lb[1])
    qh = heads(q, A_KDIM).astype(F32)
    vh = heads(v, A_VDIM).astype(F32)
    rev = lambda a: a[:, ::-1]
    q2 = jnp.concatenate([qh, rev(qh)], axis=2)
    k2 = jnp.concatenate([k_f, rev(k_b)], axis=2)
    v2 = jnp.concatenate([vh, rev(vh)], axis=2)
    g2 = jnp.concatenate([logf_f, rev(logf_b)], axis=2)
    o = chunk_gla(q2, k2, v2, g2)
    o = o[:, :, :A_HEADS] + rev(o[:, :, A_HEADS:])
    o = rms_norm(o, norm_w) * jax.nn.silu(heads(g, A_VDIM).astype(F32))
    return o.reshape(Bsz, T, A_WIDTH).astype(g.dtype)


def gqa_attention(q, k, v):
    Bsz, T, Hkv, G, d = q.shape
    nb = T // Q_BLOCK
    qb = q.reshape(Bsz, nb, Q_BLOCK, Hkv, G, d).swapaxes(0, 1)
    scale = d ** -0.5

    def one(qblk):
        s = jnp.einsum('bqhgd,bkhd->bhgqk', qblk, k).astype(F32) * scale
        p = jax.nn.softmax(s, axis=-1).astype(v.dtype)
        return jnp.einsum('bhgqk,bkhd->bqhgd', p, v)

    o = lax.map(one, qb)
    return o.swapaxes(0, 1).reshape(Bsz, T, Hkv * G * d)


def mla_attention(q_nope, q_rope, k_nope, k_rope, v):
    Bsz, T, H, _ = q_nope.shape
    nb = T // Q_BLOCK
    to_blocks = lambda a: a.reshape(Bsz, nb, Q_BLOCK, *a.shape[2:]).swapaxes(0, 1)
    scale = (C_NOPE + C_ROPE) ** -0.5

    def one(args):
        qn, qr = args
        s = (jnp.einsum('bqhd,bkhd->bhqk', qn, k_nope)
             + jnp.einsum('bqhd,bkd->bhqk', qr, k_rope)).astype(F32) * scale
        p = jax.nn.softmax(s, axis=-1).astype(v.dtype)
        return jnp.einsum('bhqk,bkhd->bqhd', p, v)

    o = lax.map(one, (to_blocks(q_nope), to_blocks(q_rope)))
    return o.swapaxes(0, 1).reshape(Bsz, T, H * v.shape[-1])


def even_mixer(x, w_in, lb, hgrn_norm_w, q_norm_w, k_norm_w, w_out, tables):
    Bsz, T, _ = x.shape
    h = x @ w_in
    aq, af_f, af_b, ai, ag, bq, bk, bv = split_cols(
        h, (A_QF, A_QF, A_QF, A_WIDTH, A_WIDTH, B_WIDTH, B_KVW, B_KVW))
    a_out = hgrn2_mix(aq, af_f, af_b, ai, ag, lb, hgrn_norm_w)
    bq = bq.reshape(Bsz, T, B_KV_HEADS, B_GROUP, B_HEAD_DIM)
    bk = bk.reshape(Bsz, T, B_KV_HEADS, B_HEAD_DIM)
    bv = bv.reshape(Bsz, T, B_KV_HEADS, B_HEAD_DIM)
    bq = apply_axial_rope(rms_norm(bq, q_norm_w), tables)
    bk = apply_axial_rope(rms_norm(bk, k_norm_w), tables)
    b_out = gqa_attention(bq, bk, bv).astype(a_out.dtype)
    return jnp.concatenate([a_out, b_out], axis=-1) @ w_out


def mla_mixer(x, w_in, q_a_norm_w, w_q_b, kv_a_norm_w, w_kv_b, w_out, tables):
    Bsz, T, _ = x.shape
    cq, ckv, k_rope = split_cols(x @ w_in, (C_Q_LORA, C_KV_LORA, C_ROPE))
    q = (rms_norm(cq, q_a_norm_w) @ w_q_b).reshape(Bsz, T, C_HEADS, C_NOPE + C_ROPE)
    q_nope, q_rope = q[..., :C_NOPE], q[..., C_NOPE:]
    kv = (rms_norm(ckv, kv_a_norm_w) @ w_kv_b).reshape(Bsz, T, C_HEADS, C_NOPE + C_V)
    k_nope, v = kv[..., :C_NOPE], kv[..., C_NOPE:]
    q_rope = apply_axial_rope(q_rope, tables)
    k_rope = apply_axial_rope(k_rope, tables)
    o = mla_attention(q_nope, q_rope, k_nope, k_rope, v)
    return o @ w_out


def swiglu(x, wg, wu, wd):
    return (jax.nn.silu(x @ wg) * (x @ wu)) @ wd


def routed_experts(xt, top_e, top_w, w_gate, w_up, w_down):
    N, D = xt.shape
    A = N * TOP_K
    flat_e = top_e.reshape(-1)
    flat_tok = jnp.repeat(jnp.arange(N, dtype=jnp.int32), TOP_K)
    flat_w = top_w.reshape(-1)
    order = jnp.argsort(flat_e)
    e_sorted = flat_e[order]
    counts = jnp.bincount(flat_e, length=N_EXPERTS)
    padded = (counts + MOE_BLOCK - 1) // MOE_BLOCK * MOE_BLOCK
    start = jnp.cumsum(counts) - counts
    pend = jnp.cumsum(padded)
    pstart = pend - padded
    dest = pstart[e_sorted] + (jnp.arange(A) - start[e_sorted])
    P = A + N_EXPERTS * MOE_BLOCK
    nblk = P // MOE_BLOCK
    slot_tok = jnp.zeros((P,), jnp.int32).at[dest].set(flat_tok[order])
    slot_w = jnp.zeros((P,), F32).at[dest].set(flat_w[order])
    blk_e = jnp.minimum(jnp.searchsorted(pend, jnp.arange(nblk) * MOE_BLOCK, side='right'), N_EXPERTS - 1)

    def body(y, blk):
        tok, e, wts = blk
        xb = xt[tok]
        yb = swiglu(xb, w_gate[e], w_up[e], w_down[e])
        return y.at[tok].add((yb * wts[:, None]).astype(y.dtype)), None

    y, _ = lax.scan(body, jnp.zeros_like(xt),
                    (slot_tok.reshape(nblk, MOE_BLOCK), blk_e, slot_w.reshape(nblk, MOE_BLOCK)))
    return y


def moe(x, w_router, router_bias, w_gate, w_up, w_down, ws_gate, ws_up, ws_down):
    Bsz, T, D = x.shape
    xt = x.reshape(-1, D)
    N = xt.shape[0]
    scores = jax.nn.sigmoid((xt @ w_router).astype(F32))
    choice = scores + router_bias.astype(F32)
    grp = choice.reshape(N, N_GROUPS, N_EXPERTS // N_GROUPS)
    grp_score = lax.top_k(grp, 2)[0].sum(-1)
    _, top_g = lax.top_k(grp_score, TOPK_GROUPS)
    gmask = jnp.any(top_g[:, :, None] == jnp.arange(N_GROUPS)[None, None, :], axis=1)
    emask = jnp.repeat(gmask, N_EXPERTS // N_GROUPS, axis=1)
    _, top_e = lax.top_k(jnp.where(emask, choice, -jnp.inf), TOP_K)
    w = jnp.take_along_axis(scores, top_e, axis=1)
    w = w / (w.sum(-1, keepdims=True) + 1e-20) * ROUTED_SCALE
    routed = routed_experts(xt, top_e, w, w_gate, w_up, w_down)
    shared = swiglu(xt, ws_gate, ws_up, ws_down)
    return (routed + shared).reshape(Bsz, T, D)


def trunk(x, params):
    (w_in_even, lb_logits, hgrn_norm_w, q_norm_w, k_norm_w, w_out_even,
     w_in_odd, q_a_norm_w, w_q_b, kv_a_norm_w, w_kv_b, w_out_odd,
     w_router, router_bias, w_gate, w_up, w_down, ws_gate, ws_up, ws_down,
     ln_mix_g, ln_mix_b, ln_ffn_g, ln_ffn_b) = params
    T = x.shape[1]
    tables_b = axial_rope_tables(T, B_HEAD_DIM)
    tables_c = axial_rope_tables(T, C_ROPE)
    lb_all = jnp.cumsum(jax.nn.softmax(lb_logits.astype(F32), axis=0), axis=0)
    for l in range(DEPTH):
        j = l // 2
        if l % 2 == 0:
            m = even_mixer(x, w_in_even[j], lb_all[j], hgrn_norm_w[j], q_norm_w[j], k_norm_w[j],
                           w_out_even[j], tables_b)
        else:
            m = mla_mixer(x, w_in_odd[j], q_a_norm_w[j], w_q_b[j], kv_a_norm_w[j], w_kv_b[j],
                          w_out_odd[j], tables_c)
        x = layer_norm(ALPHA * x + m.astype(x.dtype), ln_mix_g[l], ln_mix_b[l])
        f = moe(x, w_router[l], router_bias[l], w_gate[l], w_up[l], w_down[l],
                ws_gate[l], ws_up[l], ws_down[l])
        x = layer_norm(ALPHA * x + f.astype(x.dtype), ln_ffn_g[l], ln_ffn_b[l])
    return x


def setup_inputs(seed: int = 0) -> dict:
    key = jax.random.key(seed)
    ks = iter(jax.random.split(key, 32))
    nrm = lambda shape, scale: jax.random.normal(next(ks), shape, F32) * scale
    D = D_MODEL
    return {
        "x_prompt": nrm((BATCH, SEQ, D), 1.0),
        "x_sample": nrm((DEC_BATCH, DEC_SEQ, D), 1.0),
        "w_in_even": nrm((N_EVEN, D, EVEN_IN), D ** -0.5),
        "lb_logits": nrm((N_EVEN + 1, 2 * A_QF), 0.1),
        "hgrn_norm_w": 1.0 + nrm((N_EVEN, A_VDIM), 0.02),
        "q_norm_w": 1.0 + nrm((N_EVEN, B_HEAD_DIM), 0.02),
        "k_norm_w": 1.0 + nrm((N_EVEN, B_HEAD_DIM), 0.02),
        "w_out_even": nrm((N_EVEN, EVEN_MIX, D), EVEN_MIX ** -0.5 * BETA),
        "w_in_odd": nrm((N_ODD, D, ODD_IN), D ** -0.5),
        "q_a_norm_w": 1.0 + nrm((N_ODD, C_Q_LORA), 0.02),
        "w_q_b": nrm((N_ODD, C_Q_LORA, C_HEADS * (C_NOPE + C_ROPE)), C_Q_LORA ** -0.5),
        "kv_a_norm_w": 1.0 + nrm((N_ODD, C_KV_LORA), 0.02),
        "w_kv_b": nrm((N_ODD, C_KV_LORA, C_HEADS * (C_NOPE + C_V)), C_KV_LORA ** -0.5),
        "w_out_odd": nrm((N_ODD, C_WIDTH, D), C_WIDTH ** -0.5 * BETA),
        "w_router": nrm((DEPTH, D, N_EXPERTS), D ** -0.5),
        "router_bias": nrm((DEPTH, N_EXPERTS), 0.01),
        "w_gate": nrm((DEPTH, N_EXPERTS, D, D_EXPERT), D ** -0.5),
        "w_up": nrm((DEPTH, N_EXPERTS, D, D_EXPERT), D ** -0.5),
        "w_down": nrm((DEPTH, N_EXPERTS, D_EXPERT, D), D_EXPERT ** -0.5 * BETA),
        "ws_gate": nrm((DEPTH, D, D_SHARED), D ** -0.5),
        "ws_up": nrm((DEPTH, D, D_SHARED), D ** -0.5),
        "ws_down": nrm((DEPTH, D_SHARED, D), D_SHARED ** -0.5 * BETA),
        "ln_mix_g": 1.0 + nrm((DEPTH, D), 0.02),
        "ln_mix_b": nrm((DEPTH, D), 0.02),
        "ln_ffn_g": 1.0 + nrm((DEPTH, D), 0.02),
        "ln_ffn_b": nrm((DEPTH, D), 0.02),
    }


def reference(x_prompt, x_sample, w_in_even, lb_logits, hgrn_norm_w, q_norm_w, k_norm_w, w_out_even,
              w_in_odd, q_a_norm_w, w_q_b, kv_a_norm_w, w_kv_b, w_out_odd,
              w_router, router_bias, w_gate, w_up, w_down, ws_gate, ws_up, ws_down,
              ln_mix_g, ln_mix_b, ln_ffn_g, ln_ffn_b):
    params = (w_in_even, lb_logits, hgrn_norm_w, q_norm_w, k_norm_w, w_out_even,
              w_in_odd, q_a_norm_w, w_q_b, kv_a_norm_w, w_kv_b, w_out_odd,
              w_router, router_bias, w_gate, w_up, w_down, ws_gate, ws_up, ws_down,
              ln_mix_g, ln_mix_b, ln_ffn_g, ln_ffn_b)
    y_prompt = trunk(x_prompt, params)
    y_sample = trunk(x_sample, params)
    return (y_prompt, y_sample)
```

```python
import functools

import jax
import jax.numpy as jnp
from jax import lax
from jax.experimental import pallas as pl
from jax.experimental.pallas import tpu as pltpu

F32 = jnp.float32
BF16 = jnp.bfloat16

GRID_W = 64
ROPE_THETA = 10000.0
RMS_EPS = 1e-6
LN_EPS = 1e-5
DEPTH = 2
ALPHA = (2 * DEPTH) ** 0.25

A_HEADS = 4
A_DIM = 128
A_WIDTH = A_HEADS * A_DIM
B_HEADS = 4
B_KV_HEADS = 2
B_GROUP = B_HEADS // B_KV_HEADS
B_HEAD_DIM = 128
C_HEADS = 8
C_NOPE = 128
C_ROPE = 64
C_V = 128
C_Q_LORA = 384
C_KV_LORA = 256
N_EXPERTS = 256
TOP_K = 8
N_GROUPS = 8
TOPK_GROUPS = 4
ROUTED_SCALE = 2.5

V7X_VMEM_LIMIT_BYTES = 56 * 1024 * 1024
LANES = 128

GLA_CHUNK = 64
GLA_SUB = 16
MOE_ROWS = 512
NEG_BIG = -1e30


def _params(*semantics):
    return pltpu.CompilerParams(dimension_semantics=semantics,
                                vmem_limit_bytes=V7X_VMEM_LIMIT_BYTES)


def _tile(n, want):
    t = min(n, want)
    assert n % t == 0, (n, t)
    return t


def _mm_kernel(x_ref, w_ref, o_ref, *, tn):
    xb = x_ref[...].astype(BF16)
    n = w_ref.shape[1]
    for j in range(0, n, tn):
        w = min(tn, n - j)
        o_ref[:, j:j + w] = jnp.dot(xb, w_ref[:, j:j + w],
                                    preferred_element_type=F32).astype(o_ref.dtype)


def _matmul(x, w, out_dtype, tm=512, tn=512):
    m, k = x.shape
    n = w.shape[1]
    tm = _tile(m, tm)
    return pl.pallas_call(
        functools.partial(_mm_kernel, tn=tn),
        grid=(m // tm,),
        in_specs=[pl.BlockSpec((tm, k), lambda i: (i, 0)),
                  pl.BlockSpec((k, n), lambda i: (0, 0))],
        out_specs=pl.BlockSpec((tm, n), lambda i: (i, 0)),
        out_shape=jax.ShapeDtypeStruct((m, n), out_dtype),
        compiler_params=_params("parallel"),
    )(x, w)


def _router_kernel(x_ref, wh_ref, wl_ref, o_ref):
    x = x_ref[...]
    xh = x.astype(BF16)
    xl = (x - xh.astype(F32)).astype(BF16)
    wh = wh_ref[...]
    acc = jnp.dot(xh, wh, preferred_element_type=F32)
    acc += jnp.dot(xl, wh, preferred_element_type=F32)
    acc += jnp.dot(xh, wl_ref[...], preferred_element_type=F32)
    o_ref[...] = acc


def _router_logits(x, w):
    m, k = x.shape
    n = w.shape[1]
    wh = w.astype(BF16)
    wl = (w - wh.astype(F32)).astype(BF16)
    tm = _tile(m, 512)
    return pl.pallas_call(
        _router_kernel,
        grid=(m // tm,),
        in_specs=[pl.BlockSpec((tm, k), lambda i: (i, 0)),
                  pl.BlockSpec((k, n), lambda i: (0, 0)),
                  pl.BlockSpec((k, n), lambda i: (0, 0))],
        out_specs=pl.BlockSpec((tm, n), lambda i: (i, 0)),
        out_shape=jax.ShapeDtypeStruct((m, n), F32),
        compiler_params=_params("parallel"),
    )(x, wh, wl)


def _flash_kernel(q_ref, k_ref, v_ref, o_ref, *, groups, dq, dv, tk):
    tq = q_ref.shape[0]
    t = k_ref.shape[0]
    qs = [q_ref[:, g * dq:(g + 1) * dq] for g in range(groups)]

    def body(c, carry):
        start = pl.multiple_of(c * tk, tk)
        k = k_ref[pl.ds(start, tk), :]
        v = v_ref[pl.ds(start, tk), :]
        out = []
        for g in range(groups):
            m, l, acc = carry[g]
            s = lax.dot_general(qs[g], k, (((1,), (1,)), ((), ())),
                                preferred_element_type=F32)
            m_new = jnp.maximum(m, jnp.max(s, axis=-1, keepdims=True))
            alpha = jnp.exp(m - m_new)
            p = jnp.exp(s - m_new)
            l = alpha * l + jnp.sum(p, axis=-1, keepdims=True)
            acc = alpha * acc + jnp.dot(p.astype(BF16), v, preferred_element_type=F32)
            out.append((m_new, l, acc))
        return tuple(out)

    init = tuple((jnp.full((tq, 1), NEG_BIG, F32), jnp.zeros((tq, 1), F32),
                  jnp.zeros((tq, dv), F32)) for _ in range(groups))
    res = lax.fori_loop(0, t // tk, body, init)
    for g in range(groups):
        _, l, acc = res[g]
        o_ref[:, g * dv:(g + 1) * dv] = (acc / l).astype(o_ref.dtype)


def _gqa_attention(q, k, v, tq=512, tk=512):
    b, t, _ = q.shape
    d = B_HEAD_DIM
    tq = _tile(t, tq)
    tk = _tile(t, tk)
    kern = functools.partial(_flash_kernel, groups=B_GROUP, dq=d, dv=d, tk=tk)
    return pl.pallas_call(
        kern,
        grid=(b, B_KV_HEADS, t // tq),
        in_specs=[pl.BlockSpec((None, tq, B_GROUP * d), lambda i, h, j: (i, j, h)),
                  pl.BlockSpec((None, t, d), lambda i, h, j: (i, 0, h)),
                  pl.BlockSpec((None, t, d), lambda i, h, j: (i, 0, h))],
        out_specs=pl.BlockSpec((None, tq, B_GROUP * d), lambda i, h, j: (i, j, h)),
        out_shape=jax.ShapeDtypeStruct(q.shape, BF16),
        compiler_params=_params("parallel", "parallel", "arbitrary"),
    )(q, k, v)


def _mla_attention(q, k, v, tq=512, tk=512):
    b, h, t, dq = q.shape
    tq = _tile(t, tq)
    tk = _tile(t, tk)
    kern = functools.partial(_flash_kernel, groups=1, dq=dq, dv=C_V, tk=tk)
    return pl.pallas_call(
        kern,
        grid=(b, h, t // tq),
        in_specs=[pl.BlockSpec((None, None, tq, dq), lambda i, n, j: (i, n, j, 0)),
                  pl.BlockSpec((None, None, t, dq), lambda i, n, j: (i, n, 0, 0)),
                  pl.BlockSpec((None, t, C_V), lambda i, n, j: (i, 0, n))],
        out_specs=pl.BlockSpec((None, tq, C_V), lambda i, n, j: (i, j, n)),
        out_shape=jax.ShapeDtypeStruct(v.shape, BF16),
        compiler_params=_params("parallel", "parallel", "arbitrary"),
    )(q, k, v)


def _cumsum_time(g, rev):
    c = g.shape[0]
    row = lax.broadcasted_iota(jnp.int32, (c, c), 0)
    col = lax.broadcasted_iota(jnp.int32, (c, c), 1)
    tri = jnp.where((col >= row) if rev else (col <= row), 1.0, 0.0).astype(BF16)
    g1 = g.astype(BF16)
    r1 = g - g1.astype(F32)
    g2 = r1.astype(BF16)
    g3 = (r1 - g2.astype(F32)).astype(BF16)
    out = jnp.dot(tri, g1, preferred_element_type=F32)
    out += jnp.dot(tri, g2, preferred_element_type=F32)
    out += jnp.dot(tri, g3, preferred_element_type=F32)
    return out


def _gla_chunk(q, fl, v, lb, st, ones, rev):
    cn, sub = GLA_CHUNK, GLA_SUB
    fg = lb + (1.0 - lb) * jax.nn.sigmoid(fl)
    g = jnp.log(fg)
    kk = 1.0 - fg
    b = _cumsum_time(g, rev)
    b_edge = b[0:1] if rev else b[cn - 1:cn]
    o = lax.dot_general((q * jnp.exp(b)).astype(BF16), st.astype(BF16),
                        (((1,), (1,)), ((), ())), preferred_element_type=F32)
    kdec = (kk * jnp.exp(b_edge - b)).astype(BF16)
    st_new = st * jnp.exp(b_edge) + lax.dot_general(
        v.astype(BF16), kdec, (((0,), (0,)), ((), ())), preferred_element_type=F32)

    vb = v.astype(BF16)
    t_idx = lax.broadcasted_iota(jnp.int32, (sub, 1), 0)
    pieces = []
    nsub = cn // sub
    for i in range(nsub):
        lo, hi = i * sub, (i + 1) * sub
        bi, qi, ki, vi = b[lo:hi], q[lo:hi], kk[lo:hi], v[lo:hi]
        oi = o[lo:hi]
        if (not rev and i > 0) or (rev and i < nsub - 1):
            if rev:
                ref_b, plo, phi = b[hi:hi + 1], hi, cn
            else:
                ref_b, plo, phi = b[lo - 1:lo], 0, lo
            qd = (qi * jnp.exp(bi - ref_b)).astype(BF16)
            kd = (kk[plo:phi] * jnp.exp(ref_b - b[plo:phi])).astype(BF16)
            a = lax.dot_general(qd, kd, (((1,), (1,)), ((), ())), preferred_element_type=F32)
            oi = oi + jnp.dot(a.astype(BF16), vb[plo:phi], preferred_element_type=F32)
        terms = []
        for s in range(sub):
            d = bi - bi[s:s + 1]
            keep = (t_idx <= s) if rev else (t_idx >= s)
            d = jnp.where(keep, d, NEG_BIG)
            terms.append((qi * jnp.exp(d) * ki[s:s + 1]).astype(BF16))
        a = jnp.dot(jnp.concatenate(terms, axis=0), ones, preferred_element_type=F32)
        for s in range(sub):
            oi = oi + a[s * sub:(s + 1) * sub] * vi[s:s + 1]
        pieces.append(oi)
    return jnp.concatenate(pieces, axis=0), st_new


def _hgrn_kernel(qf_ref, ff_ref, vf_ref, qb_ref, fb_ref, vb_ref, lb_ref, of_ref, ob_ref, st_ref):
    @pl.when(pl.program_id(1) == 0)
    def _():
        st_ref[...] = jnp.zeros_like(st_ref)

    ones = jnp.ones((A_DIM, LANES), BF16)
    dirs = ((qf_ref, ff_ref, vf_ref, of_ref, False), (qb_ref, fb_ref, vb_ref, ob_ref, True))
    for di, (q_ref, f_ref, v_ref, o_ref, rev) in enumerate(dirs):
        for h in range(A_HEADS):
            sl = slice(h * A_DIM, (h + 1) * A_DIM)
            o, st = _gla_chunk(q_ref[0, :, sl], f_ref[0, :, sl], v_ref[0, :, sl],
                               lb_ref[di:di + 1, sl], st_ref[di, h], ones, rev)
            o_ref[0, :, sl] = o
            st_ref[di, h] = st


def _hgrn_scan(h, lb):
    b, t, _ = h.shape
    nc = t // GLA_CHUNK
    blk = (1, GLA_CHUNK, A_WIDTH)
    fwd = lambda col: pl.BlockSpec(blk, lambda i, c: (i, c, col))
    bwd = lambda col: pl.BlockSpec(blk, lambda i, c: (i, nc - 1 - c, col))
    out = jax.ShapeDtypeStruct((b, t, A_WIDTH), F32)
    return pl.pallas_call(
        _hgrn_kernel,
        grid=(b, nc),
        in_specs=[fwd(0), fwd(1), fwd(3), bwd(0), bwd(2), bwd(3),
                  pl.BlockSpec((2, A_WIDTH), lambda i, c: (0, 0))],
        out_specs=[pl.BlockSpec(blk, lambda i, c: (i, c, 0)),
                   pl.BlockSpec(blk, lambda i, c: (i, nc - 1 - c, 0))],
        out_shape=[out, out],
        scratch_shapes=[pltpu.VMEM((2, A_HEADS, A_DIM, A_DIM), F32)],
        compiler_params=_params("parallel", "arbitrary"),
    )(h, h, h, h, h, h, lb)


def _swiglu_body(x, wg, wu, wd):
    g = jnp.dot(x, wg, preferred_element_type=F32)
    u = jnp.dot(x, wu, preferred_element_type=F32)
    hid = (g * jax.nn.sigmoid(g) * u).astype(BF16)
    return jnp.dot(hid, wd, preferred_element_type=F32)


def _shared_kernel(x_ref, wg_ref, wu_ref, wd_ref, o_ref):
    o_ref[...] = _swiglu_body(x_ref[...], wg_ref[...], wu_ref[...], wd_ref[...]).astype(o_ref.dtype)


def _shared_expert(xb, wg, wu, wd):
    n, d = xb.shape
    f = wg.shape[1]
    tm = _tile(n, 512)
    return pl.pallas_call(
        _shared_kernel,
        grid=(n // tm,),
        in_specs=[pl.BlockSpec((tm, d), lambda i: (i, 0)),
                  pl.BlockSpec((d, f), lambda i: (0, 0)),
                  pl.BlockSpec((d, f), lambda i: (0, 0)),
                  pl.BlockSpec((f, d), lambda i: (0, 0))],
        out_specs=pl.BlockSpec((tm, d), lambda i: (i, 0)),
        out_shape=jax.ShapeDtypeStruct((n, d), F32),
        compiler_params=_params("parallel"),
    )(xb, wg, wu, wd)


def _grouped_kernel(blk_e_ref, nused_ref, x_ref, wg_ref, wu_ref, wd_ref, o_ref):
    @pl.when(pl.program_id(0) < nused_ref[0])
    def _():
        o_ref[...] = _swiglu_body(x_ref[...], wg_ref[0], wu_ref[0], wd_ref[0]).astype(o_ref.dtype)


def _grouped_experts(x_sorted, blk_e, nused, wg, wu, wd):
    p, d = x_sorted.shape
    f = wg.shape[2]
    nblk = p // MOE_ROWS
    row = lambda i, be, nu: (jnp.minimum(i, nu[0] - 1), 0)
    exp = lambda i, be, nu: (be[jnp.minimum(i, nu[0] - 1)], 0, 0)
    return pl.pallas_call(
        _grouped_kernel,
        grid_spec=pltpu.PrefetchScalarGridSpec(
            num_scalar_prefetch=2,
            grid=(nblk,),
            in_specs=[pl.BlockSpec((MOE_ROWS, d), row),
                      pl.BlockSpec((1, d, f), exp),
                      pl.BlockSpec((1, d, f), exp),
                      pl.BlockSpec((1, f, d), exp)],
            out_specs=pl.BlockSpec((MOE_ROWS, d), row)),
        out_shape=jax.ShapeDtypeStruct((p, d), BF16),
        compiler_params=_params("arbitrary"),
    )(blk_e, nused, x_sorted, wg, wu, wd)


def _rms_norm(x, w):
    return x * lax.rsqrt(jnp.mean(x * x, axis=-1, keepdims=True) + RMS_EPS) * w


def _layer_norm(x, g, b):
    mu = jnp.mean(x, axis=-1, keepdims=True)
    xc = x - mu
    var = jnp.mean(xc * xc, axis=-1, keepdims=True)
    return xc * lax.rsqrt(var + LN_EPS) * g + b


def _rope_tables(t, rot_dim):
    row = (jnp.arange(t) // GRID_W).astype(F32)
    col = (jnp.arange(t) % GRID_W).astype(F32)
    half = rot_dim // 2
    inv = ROPE_THETA ** (-jnp.arange(0, half, 2, dtype=F32) / half)
    ang_r = row[:, None] * inv[None, :]
    ang_c = col[:, None] * inv[None, :]
    return jnp.cos(ang_r), jnp.sin(ang_r), jnp.cos(ang_c), jnp.sin(ang_c)


def _axial_rope(x, tables):
    cr, sr, cc, sc = tables
    extra = x.ndim - 3
    ex = lambda a: a.reshape(a.shape[0], *([1] * extra), a.shape[1])

    def rot(y, cos, sin):
        y1, y2 = jnp.split(y, 2, axis=-1)
        return jnp.concatenate([y1 * cos - y2 * sin, y1 * sin + y2 * cos], axis=-1)

    half = x.shape[-1] // 2
    return jnp.concatenate([rot(x[..., :half], ex(cr), ex(sr)),
                            rot(x[..., half:], ex(cc), ex(sc))], axis=-1)


def _even_mixer(x, shapes, w_in, lb, hgrn_norm_w, q_norm_w, k_norm_w, w_out):
    h = _matmul(x, w_in.astype(BF16), F32)
    mixes = []
    off = 0
    for (b, t) in shapes:
        ht = h[off:off + b * t].reshape(b, t, -1)
        off += b * t
        o_f, o_b = _hgrn_scan(ht, lb.reshape(2, A_WIDTH))
        o = (o_f + o_b).reshape(b, t, A_HEADS, A_DIM)
        gate = ht[..., 4 * A_WIDTH:5 * A_WIDTH]
        a_out = (_rms_norm(o, hgrn_norm_w) * jax.nn.silu(gate.reshape(b, t, A_HEADS, A_DIM)))
        a_out = a_out.reshape(b, t, A_WIDTH)

        tables = _rope_tables(t, B_HEAD_DIM)
        c0 = 5 * A_WIDTH
        bq = ht[..., c0:c0 + 512].reshape(b, t, B_HEADS, B_HEAD_DIM)
        bk = ht[..., c0 + 512:c0 + 768].reshape(b, t, B_KV_HEADS, B_HEAD_DIM)
        bv = ht[..., c0 + 768:c0 + 1024]
        bq = _axial_rope(_rms_norm(bq, q_norm_w), tables) * (B_HEAD_DIM ** -0.5)
        bk = _axial_rope(_rms_norm(bk, k_norm_w), tables)
        b_out = _gqa_attention(bq.reshape(b, t, -1).astype(BF16), bk.reshape(b, t, -1).astype(BF16),
                               bv.astype(BF16))
        mixes.append(jnp.concatenate([a_out.astype(BF16), b_out], axis=-1).reshape(b * t, -1))
    return _matmul(jnp.concatenate(mixes, axis=0), w_out.astype(BF16), F32)


def _mla_mixer(x, shapes, w_in, q_a_norm_w, w_q_b, kv_a_norm_w, w_kv_b, w_out):
    hc = _matmul(x, w_in.astype(BF16), F32)
    cq = _rms_norm(hc[:, :C_Q_LORA], q_a_norm_w)
    ckv = _rms_norm(hc[:, C_Q_LORA:C_Q_LORA + C_KV_LORA], kv_a_norm_w)
    k_rope = hc[:, C_Q_LORA + C_KV_LORA:]
    wkv = w_kv_b.reshape(C_KV_LORA, C_HEADS, C_NOPE + C_V)
    wkv = jnp.concatenate([wkv[:, :, :C_NOPE].reshape(C_KV_LORA, -1),
                           wkv[:, :, C_NOPE:].reshape(C_KV_LORA, -1)], axis=1)
    q = _matmul(cq, w_q_b.astype(BF16), F32)
    kv = _matmul(ckv, wkv.astype(BF16), BF16)
    scale = (C_NOPE + C_ROPE) ** -0.5
    outs = []
    off = 0
    for (b, t) in shapes:
        sl = slice(off, off + b * t)
        off += b * t
        tables = _rope_tables(t, C_ROPE)
        qt = q[sl].reshape(b, t, C_HEADS, C_NOPE + C_ROPE)
        q_rope = _axial_rope(qt[..., C_NOPE:], tables)
        qc = (jnp.concatenate([qt[..., :C_NOPE], q_rope], axis=-1) * scale).astype(BF16)
        qc = qc.transpose(0, 2, 1, 3)
        kr = _axial_rope(k_rope[sl].reshape(b, t, C_ROPE), tables).astype(BF16)
        k_nope = kv[sl, :C_HEADS * C_NOPE].reshape(b, t, C_HEADS, C_NOPE)
        kc = jnp.concatenate([k_nope, jnp.broadcast_to(kr[:, :, None, :], (b, t, C_HEADS, C_ROPE))],
                             axis=-1).transpose(0, 2, 1, 3)
        v = kv[sl, C_HEADS * C_NOPE:].reshape(b, t, C_HEADS * C_V)
        outs.append(_mla_attention(qc, kc, v).reshape(b * t, -1))
    return _matmul(jnp.concatenate(outs, axis=0), w_out.astype(BF16), F32)


def _route(x, w_router, router_bias):
    n = x.shape[0]
    scores = jax.nn.sigmoid(_router_logits(x, w_router))
    choice = scores + router_bias
    grp = choice.reshape(n, N_GROUPS, N_EXPERTS // N_GROUPS)
    grp_score = lax.top_k(grp, 2)[0].sum(-1)
    _, top_g = lax.top_k(grp_score, TOPK_GROUPS)
    gmask = jnp.any(top_g[:, :, None] == jnp.arange(N_GROUPS)[None, None, :], axis=1)
    emask = jnp.repeat(gmask, N_EXPERTS // N_GROUPS, axis=1)
    _, top_e = lax.top_k(jnp.where(emask, choice, -jnp.inf), TOP_K)
    w = jnp.take_along_axis(scores, top_e, axis=1)
    w = w / (w.sum(-1, keepdims=True) + 1e-20) * ROUTED_SCALE
    return top_e, w


def _moe(x, w_router, router_bias, w_gate, w_up, w_down, ws_gate, ws_up, ws_down):
    n, d = x.shape
    top_e, top_w = _route(x, w_router, router_bias)
    xb = x.astype(BF16)
    a = n * TOP_K
    flat_e = top_e.reshape(-1)
    order = jnp.argsort(flat_e)
    e_sorted = flat_e[order]
    counts = jnp.bincount(flat_e, length=N_EXPERTS)
    padded = (counts + MOE_ROWS - 1) // MOE_ROWS * MOE_ROWS
    start = jnp.cumsum(counts) - counts
    pend = jnp.cumsum(padded)
    pstart = pend - padded
    dest_sorted = pstart[e_sorted] + (jnp.arange(a) - start[e_sorted])
    p = (a // MOE_ROWS + N_EXPERTS) * MOE_ROWS
    nblk = p // MOE_ROWS
    slot_tok = jnp.zeros((p,), jnp.int32).at[dest_sorted].set((order // TOP_K).astype(jnp.int32))
    dest = jnp.zeros((a,), jnp.int32).at[order].set(dest_sorted.astype(jnp.int32))
    blk_e = jnp.minimum(jnp.searchsorted(pend, jnp.arange(nblk) * MOE_ROWS, side='right'),
                        N_EXPERTS - 1).astype(jnp.int32)
    nused = (pend[-1] // MOE_ROWS).astype(jnp.int32).reshape(1)
    x_sorted = jnp.take(xb, slot_tok, axis=0)
    y_sorted = _grouped_experts(x_sorted, blk_e, nused, w_gate.astype(BF16), w_up.astype(BF16),
                                w_down.astype(BF16))
    y_tok = jnp.take(y_sorted, dest, axis=0).reshape(n, TOP_K, d).astype(F32)
    routed = jnp.sum(y_tok * top_w[:, :, None], axis=1)
    shared = _shared_expert(xb, ws_gate.astype(BF16), ws_up.astype(BF16), ws_down.astype(BF16))
    return routed + shared


def _trunks(xs, params):
    (w_in_even, lb_logits, hgrn_norm_w, q_norm_w, k_norm_w, w_out_even,
     w_in_odd, q_a_norm_w, w_q_b, kv_a_norm_w, w_kv_b, w_out_odd,
     w_router, router_bias, w_gate, w_up, w_down, ws_gate, ws_up, ws_down,
     ln_mix_g, ln_mix_b, ln_ffn_g, ln_ffn_b) = params
    shapes = [(x.shape[0], x.shape[1]) for x in xs]
    d = xs[0].shape[-1]
    x = jnp.concatenate([a.reshape(-1, d) for a in xs], axis=0)
    lb_all = jnp.cumsum(jax.nn.softmax(lb_logits.astype(F32), axis=0), axis=0)
    for l in range(DEPTH):
        j = l // 2
        if l % 2 == 0:
            m = _even_mixer(x, shapes, w_in_even[j], lb_all[j], hgrn_norm_w[j], q_norm_w[j],
                            k_norm_w[j], w_out_even[j])
        else:
            m = _mla_mixer(x, shapes, w_in_odd[j], q_a_norm_w[j], w_q_b[j], kv_a_norm_w[j],
                           w_kv_b[j], w_out_odd[j])
        x = _layer_norm(ALPHA * x + m, ln_mix_g[l], ln_mix_b[l])
        f = _moe(x, w_router[l], router_bias[l], w_gate[l], w_up[l], w_down[l],
                 ws_gate[l], ws_up[l], ws_down[l])
        x = _layer_norm(ALPHA * x + f, ln_ffn_g[l], ln_ffn_b[l])
    outs = []
    off = 0
    for (b, t) in shapes:
        outs.append(x[off:off + b * t].reshape(b, t, d))
        off += b * t
    return tuple(outs)


def kernel(x_prompt, x_sample, w_in_even, lb_logits, hgrn_norm_w, q_norm_w, k_norm_w, w_out_even,
           w_in_odd, q_a_norm_w, w_q_b, kv_a_norm_w, w_kv_b, w_out_odd,
           w_router, router_bias, w_gate, w_up, w_down, ws_gate, ws_up, ws_down,
           ln_mix_g, ln_mix_b, ln_ffn_g, ln_ffn_b):
    params = (w_in_even, lb_logits, hgrn_norm_w, q_norm_w, k_norm_w, w_out_even,
              w_in_odd, q_a_norm_w, w_q_b, kv_a_norm_w, w_kv_b, w_out_odd,
              w_router, router_bias, w_gate, w_up, w_down, ws_gate, ws_up, ws_down,
              ln_mix_g, ln_mix_b, ln_ffn_g, ln_ffn_b)
    return _trunks([x_prompt, x_sample], params)
```

```python
import functools

import jax
import jax.numpy as jnp
from jax import lax
from jax.experimental import pallas as pl
from jax.experimental.pallas import tpu as pltpu

F32 = jnp.float32
BF16 = jnp.bfloat16

GRID_W = 64
ROPE_THETA = 10000.0
RMS_EPS = 1e-6
LN_EPS = 1e-5
DEPTH = 2
ALPHA = (2 * DEPTH) ** 0.25

A_HEADS = 4
A_DIM = 128
A_WIDTH = A_HEADS * A_DIM
B_HEADS = 4
B_KV_HEADS = 2
B_GROUP = B_HEADS // B_KV_HEADS
B_HEAD_DIM = 128
C_HEADS = 8
C_NOPE = 128
C_ROPE = 64
C_V = 128
C_Q_LORA = 384
C_KV_LORA = 256
N_EXPERTS = 256
TOP_K = 8
N_GROUPS = 8
TOPK_GROUPS = 4
ROUTED_SCALE = 2.5

V7X_VMEM_LIMIT_BYTES = 56 * 1024 * 1024
LANES = 128

GLA_CHUNK = 64
GLA_SUB = 16
MOE_ROWS = 512
NEG_BIG = -1e30


def _params(*semantics):
    return pltpu.CompilerParams(dimension_semantics=semantics,
                                vmem_limit_bytes=V7X_VMEM_LIMIT_BYTES)


def _tile(n, want):
    t = min(n, want)
    assert n % t == 0, (n, t)
    return t


def _mm_kernel(x_ref, w_ref, o_ref, *, tn):
    xb = x_ref[...].astype(BF16)
    n = w_ref.shape[1]
    for j in range(0, n, tn):
        w = min(tn, n - j)
        o_ref[:, j:j + w] = jnp.dot(xb, w_ref[:, j:j + w],
                                    preferred_element_type=F32).astype(o_ref.dtype)


def _matmul(x, w, out_dtype, tm=512, tn=512):
    m, k = x.shape
    n = w.shape[1]
    tm = _tile(m, tm)
    return pl.pallas_call(
        functools.partial(_mm_kernel, tn=tn),
        grid=(m // tm,),
        in_specs=[pl.BlockSpec((tm, k), lambda i: (i, 0)),
                  pl.BlockSpec((k, n), lambda i: (0, 0))],
        out_specs=pl.BlockSpec((tm, n), lambda i: (i, 0)),
        out_shape=jax.ShapeDtypeStruct((m, n), out_dtype),
        compiler_params=_params("parallel"),
    )(x, w)


def _mm_ln_kernel(x_ref, w_ref, r_ref, g_ref, b_ref, o_ref, ob_ref):
    y = ALPHA * r_ref[...] + jnp.dot(x_ref[...], w_ref[...], preferred_element_type=F32)
    out = _layer_norm(y, g_ref[...], b_ref[...])
    o_ref[...] = out
    ob_ref[...] = out.astype(BF16)


def _matmul_residual_ln(x, w, resid, g, b, tm=512):
    m, k = x.shape
    d = w.shape[1]
    tm = _tile(m, tm)
    row = lambda width: pl.BlockSpec((tm, width), lambda i: (i, 0))
    const = lambda shape: pl.BlockSpec(shape, lambda i: (0, 0))
    return pl.pallas_call(
        _mm_ln_kernel,
        grid=(m // tm,),
        in_specs=[row(k), const((k, d)), row(d), const((1, d)), const((1, d))],
        out_specs=[row(d), row(d)],
        out_shape=[jax.ShapeDtypeStruct((m, d), F32), jax.ShapeDtypeStruct((m, d), BF16)],
        compiler_params=_params("parallel"),
    )(x, w, resid, g.reshape(1, d), b.reshape(1, d))


_NT = (((1,), (1,)), ((), ()))


def _row_iota(rows, cols):
    return lax.broadcasted_iota(jnp.int32, (rows, cols), 0).astype(F32)


def _first_max(cur, iota, n):
    m = jnp.max(cur, axis=0, keepdims=True)
    idx = jnp.min(jnp.where(cur == m, iota, float(n)), axis=0, keepdims=True)
    return m, idx


def _route_kernel(x_ref, wh_ref, wl_ref, bias_ref, e_ref, w_ref, cnt_ref):
    x = x_ref[...]
    tm = x.shape[0]
    xh = x.astype(BF16)
    xl = (x - xh.astype(F32)).astype(BF16)
    wh = wh_ref[...]
    logits = lax.dot_general(wh, xh, _NT, preferred_element_type=F32)
    logits += lax.dot_general(wh, xl, _NT, preferred_element_type=F32)
    logits += lax.dot_general(wl_ref[...], xh, _NT, preferred_element_type=F32)
    scores = jax.nn.sigmoid(logits)
    choice = scores + bias_ref[...]
    gsz = N_EXPERTS // N_GROUPS
    sub_iota = _row_iota(gsz, tm)
    group_rows = []
    for g in range(N_GROUPS):
        grp = choice[g * gsz:(g + 1) * gsz]
        m1, first = _first_max(grp, sub_iota, gsz)
        m2 = jnp.max(jnp.where(sub_iota == first, -jnp.inf, grp), axis=0, keepdims=True)
        group_rows.append(m1 + m2)
    cur = jnp.concatenate(group_rows, axis=0)
    g_iota = _row_iota(N_GROUPS, tm)
    sel = jnp.zeros((N_GROUPS, tm), F32)
    for _ in range(TOPK_GROUPS):
        _, gi = _first_max(cur, g_iota, N_GROUPS)
        pick = g_iota == gi
        sel = jnp.where(pick, 1.0, sel)
        cur = jnp.where(pick, -jnp.inf, cur)
    masked = jnp.concatenate(
        [jnp.where(sel[g:g + 1] > 0.5, choice[g * gsz:(g + 1) * gsz], -jnp.inf)
         for g in range(N_GROUPS)], axis=0)
    e_iota = _row_iota(N_EXPERTS, tm)
    cur = masked
    ids, vals = [], []
    for _ in range(TOP_K):
        _, ei = _first_max(cur, e_iota, N_EXPERTS)
        pick = e_iota == ei
        vals.append(jnp.sum(jnp.where(pick, scores, 0.0), axis=0, keepdims=True))
        ids.append(ei)
        cur = jnp.where(pick, -jnp.inf, cur)
    member = jnp.where((masked > -jnp.inf) & (cur == -jnp.inf), 1.0, 0.0).astype(BF16)
    s = jnp.concatenate(vals, axis=0)
    e_ref[...] = jnp.concatenate(ids, axis=0).astype(jnp.int32)
    w_ref[...] = s / (jnp.sum(s, axis=0, keepdims=True) + 1e-20) * ROUTED_SCALE
    cnt_ref[...] = lax.dot_general(jnp.ones((8, tm), BF16), member, _NT, preferred_element_type=F32)


def _route(x, w_router, router_bias, tm):
    n, d = x.shape
    wt = w_router.T
    wh = wt.astype(BF16)
    wl = (wt - wh.astype(F32)).astype(BF16)
    e, w, cnt = pl.pallas_call(
        _route_kernel,
        grid=(n // tm,),
        in_specs=[pl.BlockSpec((tm, d), lambda i: (i, 0)),
                  pl.BlockSpec((N_EXPERTS, d), lambda i: (0, 0)),
                  pl.BlockSpec((N_EXPERTS, d), lambda i: (0, 0)),
                  pl.BlockSpec((N_EXPERTS, 1), lambda i: (0, 0))],
        out_specs=[pl.BlockSpec((TOP_K, tm), lambda i: (0, i)),
                   pl.BlockSpec((TOP_K, tm), lambda i: (0, i)),
                   pl.BlockSpec((None, 8, N_EXPERTS), lambda i: (i, 0, 0))],
        out_shape=[jax.ShapeDtypeStruct((TOP_K, n), jnp.int32),
                   jax.ShapeDtypeStruct((TOP_K, n), F32),
                   jax.ShapeDtypeStruct((n // tm, 8, N_EXPERTS), F32)],
        compiler_params=_params("parallel"),
    )(x, wh, wl, router_bias.reshape(N_EXPERTS, 1).astype(F32))
    return e, w, cnt[:, 0, :]


def _dest_kernel(e_ref, base_ref, d_ref):
    e = e_ref[...].astype(F32)
    tm = e.shape[1]
    e_iota = _row_iota(N_EXPERTS, tm)
    picks = [e_iota == e[k:k + 1] for k in range(TOP_K)]
    member = jnp.zeros((N_EXPERTS, tm), F32)
    for p in picks:
        member = jnp.where(p, 1.0, member)
    r = lax.broadcasted_iota(jnp.int32, (tm, tm), 0)
    c = lax.broadcasted_iota(jnp.int32, (tm, tm), 1)
    upper = jnp.where(r <= c, 1.0, 0.0).astype(BF16)
    prefix = jnp.dot(member.astype(BF16), upper, preferred_element_type=F32)
    val = prefix + (base_ref[...] - 1.0)
    rows = [jnp.sum(jnp.where(p, val, 0.0), axis=0, keepdims=True) for p in picks]
    d_ref[...] = jnp.concatenate(rows, axis=0).astype(jnp.int32)


def _dest_rows(top_e, tile_base, tm):
    k, n = top_e.shape
    return pl.pallas_call(
        _dest_kernel,
        grid=(n // tm,),
        in_specs=[pl.BlockSpec((k, tm), lambda i: (0, i)),
                  pl.BlockSpec((None, N_EXPERTS, 1), lambda i: (i, 0, 0))],
        out_specs=pl.BlockSpec((k, tm), lambda i: (0, i)),
        out_shape=jax.ShapeDtypeStruct((k, n), jnp.int32),
        compiler_params=_params("parallel"),
    )(top_e, tile_base.astype(F32)[:, :, None])


def _flash_kernel(q_ref, k_ref, v_ref, o_ref, *, groups, dq, dv, tk):
    tq = q_ref.shape[0]
    t = k_ref.shape[0]
    qs = [q_ref[:, g * dq:(g + 1) * dq] for g in range(groups)]

    def body(c, carry):
        start = pl.multiple_of(c * tk, tk)
        k = k_ref[pl.ds(start, tk), :]
        v = v_ref[pl.ds(start, tk), :]
        out = []
        for g in range(groups):
            m, l, acc = carry[g]
            s = lax.dot_general(qs[g], k, (((1,), (1,)), ((), ())),
                                preferred_element_type=F32)
            m_new = jnp.maximum(m, jnp.max(s, axis=-1, keepdims=True))
            alpha = jnp.exp(m - m_new)
            p = jnp.exp(s - m_new)
            l = alpha * l + jnp.sum(p, axis=-1, keepdims=True)
            acc = alpha * acc + jnp.dot(p.astype(BF16), v, preferred_element_type=F32)
            out.append((m_new, l, acc))
        return tuple(out)

    init = tuple((jnp.full((tq, 1), NEG_BIG, F32), jnp.zeros((tq, 1), F32),
                  jnp.zeros((tq, dv), F32)) for _ in range(groups))
    res = lax.fori_loop(0, t // tk, body, init, unroll=2)
    for g in range(groups):
        _, l, acc = res[g]
        o_ref[:, g * dv:(g + 1) * dv] = (acc / l).astype(o_ref.dtype)


def _gqa_attention(q, k, v, tq=512, tk=512):
    b, t, _ = q.shape
    d = B_HEAD_DIM
    tq = _tile(t, tq)
    tk = _tile(t, tk)
    kern = functools.partial(_flash_kernel, groups=B_GROUP, dq=d, dv=d, tk=tk)
    return pl.pallas_call(
        kern,
        grid=(b, B_KV_HEADS, t // tq),
        in_specs=[pl.BlockSpec((None, tq, B_GROUP * d), lambda i, h, j: (i, j, h)),
                  pl.BlockSpec((None, t, d), lambda i, h, j: (i, 0, h)),
                  pl.BlockSpec((None, t, d), lambda i, h, j: (i, 0, h))],
        out_specs=pl.BlockSpec((None, tq, B_GROUP * d), lambda i, h, j: (i, j, h)),
        out_shape=jax.ShapeDtypeStruct(q.shape, BF16),
        compiler_params=_params("parallel", "parallel", "arbitrary"),
    )(q, k, v)


def _mla_attention(q, k, v, tq=512, tk=512):
    b, h, t, dq = q.shape
    tq = _tile(t, tq)
    tk = _tile(t, tk)
    kern = functools.partial(_flash_kernel, groups=1, dq=dq, dv=C_V, tk=tk)
    return pl.pallas_call(
        kern,
        grid=(b, h, t // tq),
        in_specs=[pl.BlockSpec((None, None, tq, dq), lambda i, n, j: (i, n, j, 0)),
                  pl.BlockSpec((None, None, t, dq), lambda i, n, j: (i, n, 0, 0)),
                  pl.BlockSpec((None, t, C_V), lambda i, n, j: (i, 0, n))],
        out_specs=pl.BlockSpec((None, tq, C_V), lambda i, n, j: (i, j, n)),
        out_shape=jax.ShapeDtypeStruct(v.shape, BF16),
        compiler_params=_params("parallel", "parallel", "arbitrary"),
    )(q, k, v)


def _cumsum_time(g, rev):
    c = g.shape[0]
    row = lax.broadcasted_iota(jnp.int32, (c, c), 0)
    col = lax.broadcasted_iota(jnp.int32, (c, c), 1)
    tri = jnp.where((col >= row) if rev else (col <= row), 1.0, 0.0).astype(BF16)
    g1 = g.astype(BF16)
    r1 = g - g1.astype(F32)
    g2 = r1.astype(BF16)
    g3 = (r1 - g2.astype(F32)).astype(BF16)
    out = jnp.dot(tri, g1, preferred_element_type=F32)
    out += jnp.dot(tri, g2, preferred_element_type=F32)
    out += jnp.dot(tri, g3, preferred_element_type=F32)
    return out


def _gla_chunk(q, fl, v, lb, st, ones, rev):
    cn, sub = GLA_CHUNK, GLA_SUB
    fg = lb + (1.0 - lb) * jax.nn.sigmoid(fl)
    g = jnp.log(fg)
    kk = 1.0 - fg
    b = _cumsum_time(g, rev)
    b_edge = b[0:1] if rev else b[cn - 1:cn]
    o = lax.dot_general((q * jnp.exp(b)).astype(BF16), st.astype(BF16),
                        (((1,), (1,)), ((), ())), preferred_element_type=F32)
    kdec = (kk * jnp.exp(b_edge - b)).astype(BF16)
    st_new = st * jnp.exp(b_edge) + lax.dot_general(
        v.astype(BF16), kdec, (((0,), (0,)), ((), ())), preferred_element_type=F32)

    vb = v.astype(BF16)
    t_idx = lax.broadcasted_iota(jnp.int32, (sub, 1), 0)
    pieces = []
    nsub = cn // sub
    for i in range(nsub):
        lo, hi = i * sub, (i + 1) * sub
        bi, qi, ki, vi = b[lo:hi], q[lo:hi], kk[lo:hi], v[lo:hi]
        oi = o[lo:hi]
        if (not rev and i > 0) or (rev and i < nsub - 1):
            if rev:
                ref_b, plo, phi = b[hi:hi + 1], hi, cn
            else:
                ref_b, plo, phi = b[lo - 1:lo], 0, lo
            qd = (qi * jnp.exp(bi - ref_b)).astype(BF16)
            kd = (kk[plo:phi] * jnp.exp(ref_b - b[plo:phi])).astype(BF16)
            a = lax.dot_general(qd, kd, (((1,), (1,)), ((), ())), preferred_element_type=F32)
            oi = oi + jnp.dot(a.astype(BF16), vb[plo:phi], preferred_element_type=F32)
        terms = []
        for s in range(sub):
            d = bi - bi[s:s + 1]
            keep = (t_idx <= s) if rev else (t_idx >= s)
            d = jnp.where(keep, d, NEG_BIG)
            terms.append((qi * jnp.exp(d) * ki[s:s + 1]).astype(BF16))
        a = jnp.dot(jnp.concatenate(terms, axis=0), ones, preferred_element_type=F32)
        for s in range(sub):
            oi = oi + a[s * sub:(s + 1) * sub] * vi[s:s + 1]
        pieces.append(oi)
    return jnp.concatenate(pieces, axis=0), st_new


def _hgrn_kernel(qf_ref, ff_ref, vf_ref, qb_ref, fb_ref, vb_ref, lb_ref, of_ref, ob_ref, st_ref):
    @pl.when(pl.program_id(1) == 0)
    def _():
        st_ref[...] = jnp.zeros_like(st_ref)

    ones = jnp.ones((A_DIM, LANES), BF16)
    dirs = ((qf_ref, ff_ref, vf_ref, of_ref, False), (qb_ref, fb_ref, vb_ref, ob_ref, True))
    for di, (q_ref, f_ref, v_ref, o_ref, rev) in enumerate(dirs):
        for h in range(A_HEADS):
            sl = slice(h * A_DIM, (h + 1) * A_DIM)
            o, st = _gla_chunk(q_ref[0, :, sl], f_ref[0, :, sl], v_ref[0, :, sl],
                               lb_ref[di:di + 1, sl], st_ref[di, h], ones, rev)
            o_ref[0, :, sl] = o
            st_ref[di, h] = st


def _hgrn_scan(h, lb):
    b, t, _ = h.shape
    nc = t // GLA_CHUNK
    blk = (1, GLA_CHUNK, A_WIDTH)
    fwd = lambda col: pl.BlockSpec(blk, lambda i, c: (i, c, col))
    bwd = lambda col: pl.BlockSpec(blk, lambda i, c: (i, nc - 1 - c, col))
    out = jax.ShapeDtypeStruct((b, t, A_WIDTH), F32)
    return pl.pallas_call(
        _hgrn_kernel,
        grid=(b, nc),
        in_specs=[fwd(0), fwd(1), fwd(3), bwd(0), bwd(2), bwd(3),
                  pl.BlockSpec((2, A_WIDTH), lambda i, c: (0, 0))],
        out_specs=[pl.BlockSpec(blk, lambda i, c: (i, c, 0)),
                   pl.BlockSpec(blk, lambda i, c: (i, nc - 1 - c, 0))],
        out_shape=[out, out],
        scratch_shapes=[pltpu.VMEM((2, A_HEADS, A_DIM, A_DIM), F32)],
        compiler_params=_params("parallel", "arbitrary"),
    )(h, h, h, h, h, h, lb)


def _swiglu_body(x, wg, wu, wd):
    g = jnp.dot(x, wg, preferred_element_type=F32)
    u = jnp.dot(x, wu, preferred_element_type=F32)
    hid = (g * jax.nn.sigmoid(g) * u).astype(BF16)
    return jnp.dot(hid, wd, preferred_element_type=F32)


def _combine_kernel(y_ref, w_ref, x_ref, xb_ref, wg_ref, wu_ref, wd_ref, g_ref, b_ref, o_ref, ob_ref):
    w = w_ref[...]
    f = _swiglu_body(xb_ref[...], wg_ref[...], wu_ref[...], wd_ref[...])
    for k in range(TOP_K):
        f = f + y_ref[k].astype(F32) * w[:, k:k + 1]
    out = _layer_norm(ALPHA * x_ref[...] + f, g_ref[...], b_ref[...])
    o_ref[...] = out
    ob_ref[...] = out.astype(BF16)


def _moe_combine(y_tok, w_tok, x, xb, wg, wu, wd, g, b, tm=256):
    n, d = x.shape
    f = wg.shape[1]
    tm = _tile(n, tm)
    row = lambda width: pl.BlockSpec((tm, width), lambda i: (i, 0))
    const = lambda shape: pl.BlockSpec(shape, lambda i: (0, 0))
    return pl.pallas_call(
        _combine_kernel,
        grid=(n // tm,),
        in_specs=[pl.BlockSpec((TOP_K, tm, d), lambda i: (0, i, 0)), row(TOP_K), row(d), row(d),
                  const((d, f)), const((d, f)), const((f, d)), const((1, d)), const((1, d))],
        out_specs=[row(d), row(d)],
        out_shape=[jax.ShapeDtypeStruct((n, d), F32), jax.ShapeDtypeStruct((n, d), BF16)],
        compiler_params=_params("parallel"),
    )(y_tok, w_tok, x, xb, wg, wu, wd, g.reshape(1, d), b.reshape(1, d))


def _grouped_kernel(blk_e_ref, nused_ref, x_ref, wg_ref, wu_ref, wd_ref, o_ref):
    @pl.when(pl.program_id(0) < nused_ref[0])
    def _():
        o_ref[...] = _swiglu_body(x_ref[...], wg_ref[0], wu_ref[0], wd_ref[0]).astype(o_ref.dtype)


def _grouped_experts(x_sorted, blk_e, nused, wg, wu, wd):
    p, d = x_sorted.shape
    f = wg.shape[2]
    nblk = p // MOE_ROWS
    row = lambda i, be, nu: (jnp.minimum(i, nu[0] - 1), 0)
    exp = lambda i, be, nu: (be[jnp.minimum(i, nu[0] - 1)], 0, 0)
    return pl.pallas_call(
        _grouped_kernel,
        grid_spec=pltpu.PrefetchScalarGridSpec(
            num_scalar_prefetch=2,
            grid=(nblk,),
            in_specs=[pl.BlockSpec((MOE_ROWS, d), row),
                      pl.BlockSpec((1, d, f), exp),
                      pl.BlockSpec((1, d, f), exp),
                      pl.BlockSpec((1, f, d), exp)],
            out_specs=pl.BlockSpec((MOE_ROWS, d), row)),
        out_shape=jax.ShapeDtypeStruct((p, d), BF16),
        compiler_params=_params("arbitrary"),
    )(blk_e, nused, x_sorted, wg, wu, wd)


def _rms_norm(x, w):
    return x * lax.rsqrt(jnp.mean(x * x, axis=-1, keepdims=True) + RMS_EPS) * w


def _layer_norm(x, g, b):
    mu = jnp.mean(x, axis=-1, keepdims=True)
    xc = x - mu
    var = jnp.mean(xc * xc, axis=-1, keepdims=True)
    return xc * lax.rsqrt(var + LN_EPS) * g + b


def _rope_tables(t, rot_dim):
    row = (jnp.arange(t) // GRID_W).astype(F32)
    col = (jnp.arange(t) % GRID_W).astype(F32)
    half = rot_dim // 2
    inv = ROPE_THETA ** (-jnp.arange(0, half, 2, dtype=F32) / half)
    ang_r = row[:, None] * inv[None, :]
    ang_c = col[:, None] * inv[None, :]
    return jnp.cos(ang_r), jnp.sin(ang_r), jnp.cos(ang_c), jnp.sin(ang_c)


def _axial_rope(x, tables):
    cr, sr, cc, sc = tables
    extra = x.ndim - 3
    ex = lambda a: a.reshape(a.shape[0], *([1] * extra), a.shape[1])

    def rot(y, cos, sin):
        y1, y2 = jnp.split(y, 2, axis=-1)
        return jnp.concatenate([y1 * cos - y2 * sin, y1 * sin + y2 * cos], axis=-1)

    half = x.shape[-1] // 2
    return jnp.concatenate([rot(x[..., :half], ex(cr), ex(sr)),
                            rot(x[..., half:], ex(cc), ex(sc))], axis=-1)


def _even_mixer(x, shapes, w_in, lb, hgrn_norm_w, q_norm_w, k_norm_w):
    h = _matmul(x, w_in.astype(BF16), F32)
    mixes = []
    off = 0
    for (b, t) in shapes:
        ht = h[off:off + b * t].reshape(b, t, -1)
        off += b * t
        o_f, o_b = _hgrn_scan(ht, lb.reshape(2, A_WIDTH))
        o = (o_f + o_b).reshape(b, t, A_HEADS, A_DIM)
        gate = ht[..., 4 * A_WIDTH:5 * A_WIDTH]
        a_out = (_rms_norm(o, hgrn_norm_w) * jax.nn.silu(gate.reshape(b, t, A_HEADS, A_DIM)))
        a_out = a_out.reshape(b, t, A_WIDTH)

        tables = _rope_tables(t, B_HEAD_DIM)
        c0 = 5 * A_WIDTH
        bq = ht[..., c0:c0 + 512].reshape(b, t, B_HEADS, B_HEAD_DIM)
        bk = ht[..., c0 + 512:c0 + 768].reshape(b, t, B_KV_HEADS, B_HEAD_DIM)
        bv = ht[..., c0 + 768:c0 + 1024]
        bq = _axial_rope(_rms_norm(bq, q_norm_w), tables) * (B_HEAD_DIM ** -0.5)
        bk = _axial_rope(_rms_norm(bk, k_norm_w), tables)
        b_out = _gqa_attention(bq.reshape(b, t, -1).astype(BF16), bk.reshape(b, t, -1).astype(BF16),
                               bv.astype(BF16))
        mixes.append(jnp.concatenate([a_out.astype(BF16), b_out], axis=-1).reshape(b * t, -1))
    return jnp.concatenate(mixes, axis=0)


def _mla_mixer(x, shapes, w_in, q_a_norm_w, w_q_b, kv_a_norm_w, w_kv_b):
    hc = _matmul(x, w_in.astype(BF16), F32)
    cq = _rms_norm(hc[:, :C_Q_LORA], q_a_norm_w)
    ckv = _rms_norm(hc[:, C_Q_LORA:C_Q_LORA + C_KV_LORA], kv_a_norm_w)
    k_rope = hc[:, C_Q_LORA + C_KV_LORA:]
    wkv = w_kv_b.reshape(C_KV_LORA, C_HEADS, C_NOPE + C_V)
    wkv = jnp.concatenate([wkv[:, :, :C_NOPE].reshape(C_KV_LORA, -1),
                           wkv[:, :, C_NOPE:].reshape(C_KV_LORA, -1)], axis=1)
    q = _matmul(cq, w_q_b.astype(BF16), F32)
    kv = _matmul(ckv, wkv.astype(BF16), BF16)
    scale = (C_NOPE + C_ROPE) ** -0.5
    outs = []
    off = 0
    for (b, t) in shapes:
        sl = slice(off, off + b * t)
        off += b * t
        tables = _rope_tables(t, C_ROPE)
        qt = q[sl].reshape(b, t, C_HEADS, C_NOPE + C_ROPE)
        q_rope = _axial_rope(qt[..., C_NOPE:], tables)
        qc = (jnp.concatenate([qt[..., :C_NOPE], q_rope], axis=-1) * scale).astype(BF16)
        qc = qc.transpose(0, 2, 1, 3)
        kr = _axial_rope(k_rope[sl].reshape(b, t, C_ROPE), tables).astype(BF16)
        k_nope = kv[sl, :C_HEADS * C_NOPE].reshape(b, t, C_HEADS, C_NOPE)
        kc = jnp.concatenate([k_nope, jnp.broadcast_to(kr[:, :, None, :], (b, t, C_HEADS, C_ROPE))],
                             axis=-1).transpose(0, 2, 1, 3)
        v = kv[sl, C_HEADS * C_NOPE:].reshape(b, t, C_HEADS * C_V)
        outs.append(_mla_attention(qc, kc, v).reshape(b * t, -1))
    return jnp.concatenate(outs, axis=0)


def _moe_block(x, xb, w_router, router_bias, w_gate, w_up, w_down, ws_gate, ws_up, ws_down, ln_g, ln_b):
    n, d = x.shape
    tm = _tile(n, 512)
    top_e, top_w, tile_cnt = _route(x, w_router, router_bias, tm)
    tile_cnt = tile_cnt.astype(jnp.int32)
    counts = jnp.sum(tile_cnt, axis=0)
    padded = (counts + MOE_ROWS - 1) // MOE_ROWS * MOE_ROWS
    pend = jnp.cumsum(padded)
    pstart = pend - padded
    tile_base = pstart[None, :] + jnp.cumsum(tile_cnt, axis=0) - tile_cnt
    dest = _dest_rows(top_e, tile_base, tm)
    nblk = (n * TOP_K) // MOE_ROWS + N_EXPERTS
    blk_e = jnp.minimum(jnp.searchsorted(pend, jnp.arange(nblk) * MOE_ROWS, side='right'),
                        N_EXPERTS - 1).astype(jnp.int32)
    nused = (pend[-1] // MOE_ROWS).astype(jnp.int32).reshape(1)
    tok = jnp.broadcast_to(jnp.arange(n, dtype=jnp.int32)[None, :], (TOP_K, n))
    slot_tok = jnp.zeros((nblk * MOE_ROWS,), jnp.int32).at[dest.reshape(-1)].set(tok.reshape(-1))
    x_sorted = jnp.take(xb, slot_tok, axis=0)
    y_sorted = _grouped_experts(x_sorted, blk_e, nused, w_gate.astype(BF16), w_up.astype(BF16),
                                w_down.astype(BF16))
    y_tok = jnp.take(y_sorted, dest.reshape(-1), axis=0).reshape(TOP_K, n, d)
    return _moe_combine(y_tok, top_w.T, x, xb, ws_gate.astype(BF16), ws_up.astype(BF16),
                        ws_down.astype(BF16), ln_g, ln_b)


def _trunks(xs, params):
    (w_in_even, lb_logits, hgrn_norm_w, q_norm_w, k_norm_w, w_out_even,
     w_in_odd, q_a_norm_w, w_q_b, kv_a_norm_w, w_kv_b, w_out_odd,
     w_router, router_bias, w_gate, w_up, w_down, ws_gate, ws_up, ws_down,
     ln_mix_g, ln_mix_b, ln_ffn_g, ln_ffn_b) = params
    shapes = [(x.shape[0], x.shape[1]) for x in xs]
    d = xs[0].shape[-1]
    x = jnp.concatenate([a.reshape(-1, d) for a in xs], axis=0)
    xb = x.astype(BF16)
    lb_all = jnp.cumsum(jax.nn.softmax(lb_logits.astype(F32), axis=0), axis=0)
    for l in range(DEPTH):
        j = l // 2
        if l % 2 == 0:
            mix = _even_mixer(xb, shapes, w_in_even[j], lb_all[j], hgrn_norm_w[j], q_norm_w[j],
                              k_norm_w[j])
            w_out = w_out_even[j]
        else:
            mix = _mla_mixer(xb, shapes, w_in_odd[j], q_a_norm_w[j], w_q_b[j], kv_a_norm_w[j],
                             w_kv_b[j])
            w_out = w_out_odd[j]
        x, xb = _matmul_residual_ln(mix, w_out.astype(BF16), x, ln_mix_g[l], ln_mix_b[l])
        x, xb = _moe_block(x, xb, w_router[l], router_bias[l], w_gate[l], w_up[l], w_down[l],
                           ws_gate[l], ws_up[l], ws_down[l], ln_ffn_g[l], ln_ffn_b[l])
    outs = []
    off = 0
    for (b, t) in shapes:
        outs.append(x[off:off + b * t].reshape(b, t, d))
        off += b * t
    return tuple(outs)


def kernel(x_prompt, x_sample, w_in_even, lb_logits, hgrn_norm_w, q_norm_w, k_norm_w, w_out_even,
           w_in_odd, q_a_norm_w, w_q_b, kv_a_norm_w, w_kv_b, w_out_odd,
           w_router, router_bias, w_gate, w_up, w_down, ws_gate, ws_up, ws_down,
           ln_mix_g, ln_mix_b, ln_ffn_g, ln_ffn_b):
    params = (w_in_even, lb_logits, hgrn_norm_w, q_norm_w, k_norm_w, w_out_even,
              w_in_odd, q_a_norm_w, w_q_b, kv_a_norm_w, w_kv_b, w_out_odd,
              w_router, router_bias, w_gate, w_up, w_down, ws_gate, ws_up, ws_down,
              ln_mix_g, ln_mix_b, ln_ffn_g, ln_ffn_b)
    return _trunks([x_prompt, x_sample], params)
```

```python
import functools

import jax
import jax.numpy as jnp
from jax import lax
from jax.experimental import pallas as pl
from jax.experimental.pallas import tpu as pltpu

F32 = jnp.float32
BF16 = jnp.bfloat16

GRID_W = 64
ROPE_THETA = 10000.0
RMS_EPS = 1e-6
LN_EPS = 1e-5
DEPTH = 2
ALPHA = (2 * DEPTH) ** 0.25

A_HEADS = 4
A_DIM = 128
A_WIDTH = A_HEADS * A_DIM
B_HEADS = 4
B_KV_HEADS = 2
B_GROUP = B_HEADS // B_KV_HEADS
B_HEAD_DIM = 128
C_HEADS = 8
C_NOPE = 128
C_ROPE = 64
C_V = 128
C_Q_LORA = 384
C_KV_LORA = 256
N_EXPERTS = 256
TOP_K = 8
N_GROUPS = 8
TOPK_GROUPS = 4
ROUTED_SCALE = 2.5

V7X_VMEM_LIMIT_BYTES = 56 * 1024 * 1024
LANES = 128
SUBLANES = 8

GLA_CHUNK = 64
GLA_SUB = 16
MOE_ROWS = 512
NEG_BIG = -1e30
LOG2_E = 1.4426950408889634


def _params(*semantics):
    return pltpu.CompilerParams(dimension_semantics=semantics,
                                vmem_limit_bytes=V7X_VMEM_LIMIT_BYTES)


def _tile(n, want):
    t = min(n, want)
    assert n % t == 0, (n, t)
    return t


def _mm_kernel(x_ref, w_ref, o_ref, *, tn):
    xb = x_ref[...].astype(BF16)
    n = w_ref.shape[1]
    for j in range(0, n, tn):
        w = min(tn, n - j)
        o_ref[:, j:j + w] = jnp.dot(xb, w_ref[:, j:j + w],
                                    preferred_element_type=F32).astype(o_ref.dtype)


def _matmul(x, w, out_dtype, tm=512, tn=512):
    m, k = x.shape
    n = w.shape[1]
    tm = _tile(m, tm)
    return pl.pallas_call(
        functools.partial(_mm_kernel, tn=tn),
        grid=(m // tm,),
        in_specs=[pl.BlockSpec((tm, k), lambda i: (i, 0)),
                  pl.BlockSpec((k, n), lambda i: (0, 0))],
        out_specs=pl.BlockSpec((tm, n), lambda i: (i, 0)),
        out_shape=jax.ShapeDtypeStruct((m, n), out_dtype),
        compiler_params=_params("parallel"),
    )(x, w)


def _mm_ln_kernel(x_ref, w_ref, r_ref, g_ref, b_ref, o_ref, ob_ref):
    y = ALPHA * r_ref[...] + jnp.dot(x_ref[...], w_ref[...], preferred_element_type=F32)
    out = _layer_norm(y, g_ref[...], b_ref[...])
    o_ref[...] = out
    ob_ref[...] = out.astype(BF16)


def _matmul_residual_ln(x, w, resid, g, b, tm=512):
    m, k = x.shape
    d = w.shape[1]
    tm = _tile(m, tm)
    row = lambda width: pl.BlockSpec((tm, width), lambda i: (i, 0))
    const = lambda shape: pl.BlockSpec(shape, lambda i: (0, 0))
    return pl.pallas_call(
        _mm_ln_kernel,
        grid=(m // tm,),
        in_specs=[row(k), const((k, d)), row(d), const((1, d)), const((1, d))],
        out_specs=[row(d), row(d)],
        out_shape=[jax.ShapeDtypeStruct((m, d), F32), jax.ShapeDtypeStruct((m, d), BF16)],
        compiler_params=_params("parallel"),
    )(x, w, resid, g.reshape(1, d), b.reshape(1, d))


_NT = (((1,), (1,)), ((), ()))


def _row_iota(rows, cols):
    return lax.broadcasted_iota(jnp.int32, (rows, cols), 0).astype(F32)


def _first_max(cur, iota, n):
    m = jnp.max(cur, axis=0, keepdims=True)
    idx = jnp.min(jnp.where(cur == m, iota, float(n)), axis=0, keepdims=True)
    return m, idx


def _route_kernel(x_ref, wh_ref, wl_ref, bias_ref, e_ref, w_ref, cnt_ref):
    x = x_ref[...]
    tm = x.shape[0]
    xh = x.astype(BF16)
    xl = (x - xh.astype(F32)).astype(BF16)
    wh = wh_ref[...]
    logits = lax.dot_general(wh, xh, _NT, preferred_element_type=F32)
    logits += lax.dot_general(wh, xl, _NT, preferred_element_type=F32)
    logits += lax.dot_general(wl_ref[...], xh, _NT, preferred_element_type=F32)
    scores = jax.nn.sigmoid(logits)
    choice = scores + bias_ref[...]
    gsz = N_EXPERTS // N_GROUPS
    sub_iota = _row_iota(gsz, tm)
    group_rows = []
    for g in range(N_GROUPS):
        grp = choice[g * gsz:(g + 1) * gsz]
        m1, first = _first_max(grp, sub_iota, gsz)
        m2 = jnp.max(jnp.where(sub_iota == first, -jnp.inf, grp), axis=0, keepdims=True)
        group_rows.append(m1 + m2)
    cur = jnp.concatenate(group_rows, axis=0)
    g_iota = _row_iota(N_GROUPS, tm)
    sel = jnp.zeros((N_GROUPS, tm), F32)
    for _ in range(TOPK_GROUPS):
        _, gi = _first_max(cur, g_iota, N_GROUPS)
        pick = g_iota == gi
        sel = jnp.where(pick, 1.0, sel)
        cur = jnp.where(pick, -jnp.inf, cur)
    masked = jnp.concatenate(
        [jnp.where(sel[g:g + 1] > 0.5, choice[g * gsz:(g + 1) * gsz], -jnp.inf)
         for g in range(N_GROUPS)], axis=0)
    e_iota = _row_iota(N_EXPERTS, tm)
    cur = masked
    ids, vals = [], []
    for _ in range(TOP_K):
        _, ei = _first_max(cur, e_iota, N_EXPERTS)
        pick = e_iota == ei
        vals.append(jnp.sum(jnp.where(pick, scores, 0.0), axis=0, keepdims=True))
        ids.append(ei)
        cur = jnp.where(pick, -jnp.inf, cur)
    member = jnp.where((masked > -jnp.inf) & (cur == -jnp.inf), 1.0, 0.0).astype(BF16)
    s = jnp.concatenate(vals, axis=0)
    e_ref[...] = jnp.concatenate(ids, axis=0).astype(jnp.int32)
    w_ref[...] = s / (jnp.sum(s, axis=0, keepdims=True) + 1e-20) * ROUTED_SCALE
    cnt_ref[...] = lax.dot_general(jnp.ones((8, tm), BF16), member, _NT, preferred_element_type=F32)


def _route(x, w_router, router_bias, tm):
    n, d = x.shape
    wt = w_router.T
    wh = wt.astype(BF16)
    wl = (wt - wh.astype(F32)).astype(BF16)
    e, w, cnt = pl.pallas_call(
        _route_kernel,
        grid=(n // tm,),
        in_specs=[pl.BlockSpec((tm, d), lambda i: (i, 0)),
                  pl.BlockSpec((N_EXPERTS, d), lambda i: (0, 0)),
                  pl.BlockSpec((N_EXPERTS, d), lambda i: (0, 0)),
                  pl.BlockSpec((N_EXPERTS, 1), lambda i: (0, 0))],
        out_specs=[pl.BlockSpec((TOP_K, tm), lambda i: (0, i)),
                   pl.BlockSpec((TOP_K, tm), lambda i: (0, i)),
                   pl.BlockSpec((None, 8, N_EXPERTS), lambda i: (i, 0, 0))],
        out_shape=[jax.ShapeDtypeStruct((TOP_K, n), jnp.int32),
                   jax.ShapeDtypeStruct((TOP_K, n), F32),
                   jax.ShapeDtypeStruct((n // tm, 8, N_EXPERTS), F32)],
        compiler_params=_params("parallel"),
    )(x, wh, wl, router_bias.reshape(N_EXPERTS, 1).astype(F32))
    return e, w, cnt[:, 0, :]


def _dest_kernel(e_ref, base_ref, d_ref):
    e = e_ref[...].astype(F32)
    tm = e.shape[1]
    e_iota = _row_iota(N_EXPERTS, tm)
    picks = [e_iota == e[k:k + 1] for k in range(TOP_K)]
    member = jnp.zeros((N_EXPERTS, tm), F32)
    for p in picks:
        member = jnp.where(p, 1.0, member)
    r = lax.broadcasted_iota(jnp.int32, (tm, tm), 0)
    c = lax.broadcasted_iota(jnp.int32, (tm, tm), 1)
    upper = jnp.where(r <= c, 1.0, 0.0).astype(BF16)
    prefix = jnp.dot(member.astype(BF16), upper, preferred_element_type=F32)
    val = prefix + (base_ref[...] - 1.0)
    rows = [jnp.sum(jnp.where(p, val, 0.0), axis=0, keepdims=True) for p in picks]
    d_ref[...] = jnp.concatenate(rows, axis=0).astype(jnp.int32)


def _dest_rows(top_e, tile_base, tm):
    k, n = top_e.shape
    return pl.pallas_call(
        _dest_kernel,
        grid=(n // tm,),
        in_specs=[pl.BlockSpec((k, tm), lambda i: (0, i)),
                  pl.BlockSpec((None, N_EXPERTS, 1), lambda i: (i, 0, 0))],
        out_specs=pl.BlockSpec((k, tm), lambda i: (0, i)),
        out_shape=jax.ShapeDtypeStruct((k, n), jnp.int32),
        compiler_params=_params("parallel"),
    )(top_e, tile_base.astype(F32)[:, :, None])


def _flash_kernel(q_ref, k_ref, v_ref, o_ref, *, groups, dq, dv, tk):
    tq = q_ref.shape[0]
    t = k_ref.shape[0]
    qs = [q_ref[:, g * dq:(g + 1) * dq] for g in range(groups)]

    def body(c, carry):
        start = pl.multiple_of(c * tk, tk)
        k = k_ref[pl.ds(start, tk), :]
        v = v_ref[pl.ds(start, tk), :]
        out = []
        for g in range(groups):
            m, l, acc = carry[g]
            s = lax.dot_general(qs[g], k, (((1,), (1,)), ((), ())),
                                preferred_element_type=F32)
            m_new = jnp.maximum(m, jnp.max(s, axis=-1, keepdims=True))
            alpha = jnp.exp2(m - m_new)
            p = jnp.exp2(s - m_new)
            l = alpha * l + jnp.sum(p, axis=-1, keepdims=True)
            acc = alpha * acc + jnp.dot(p.astype(BF16), v, preferred_element_type=F32)
            out.append((m_new, l, acc))
        return tuple(out)

    init = tuple((jnp.full((tq, 1), NEG_BIG, F32), jnp.zeros((tq, 1), F32),
                  jnp.zeros((tq, dv), F32)) for _ in range(groups))
    res = lax.fori_loop(0, t // tk, body, init, unroll=2)
    for g in range(groups):
        _, l, acc = res[g]
        o_ref[:, g * dv:(g + 1) * dv] = (acc / l).astype(o_ref.dtype)


def _gqa_attention(q, k, v, tq=512, tk=1024):
    b, t, _ = q.shape
    d = B_HEAD_DIM
    tq = _tile(t, tq)
    tk = _tile(t, tk)
    kern = functools.partial(_flash_kernel, groups=B_GROUP, dq=d, dv=d, tk=tk)
    return pl.pallas_call(
        kern,
        grid=(b, B_KV_HEADS, t // tq),
        in_specs=[pl.BlockSpec((None, tq, B_GROUP * d), lambda i, h, j: (i, j, h)),
                  pl.BlockSpec((None, t, d), lambda i, h, j: (i, 0, h)),
                  pl.BlockSpec((None, t, d), lambda i, h, j: (i, 0, h))],
        out_specs=pl.BlockSpec((None, tq, B_GROUP * d), lambda i, h, j: (i, j, h)),
        out_shape=jax.ShapeDtypeStruct(q.shape, BF16),
        compiler_params=_params("parallel", "parallel", "arbitrary"),
    )(q, k, v)


def _mla_attention(q, k, v, tq=512, tk=1024):
    b, h, t, dq = q.shape
    tq = _tile(t, tq)
    tk = _tile(t, tk)
    kern = functools.partial(_flash_kernel, groups=1, dq=dq, dv=C_V, tk=tk)
    return pl.pallas_call(
        kern,
        grid=(b, h, t // tq),
        in_specs=[pl.BlockSpec((None, None, tq, dq), lambda i, n, j: (i, n, j, 0)),
                  pl.BlockSpec((None, None, t, dq), lambda i, n, j: (i, n, 0, 0)),
                  pl.BlockSpec((None, t, C_V), lambda i, n, j: (i, 0, n))],
        out_specs=pl.BlockSpec((None, tq, C_V), lambda i, n, j: (i, j, n)),
        out_shape=jax.ShapeDtypeStruct(v.shape, BF16),
        compiler_params=_params("parallel", "parallel", "arbitrary"),
    )(q, k, v)


def _cumsum_time(g, rev):
    c = g.shape[0]
    row = lax.broadcasted_iota(jnp.int32, (c, c), 0)
    col = lax.broadcasted_iota(jnp.int32, (c, c), 1)
    tri = jnp.where((col >= row) if rev else (col <= row), 1.0, 0.0).astype(BF16)
    g1 = g.astype(BF16)
    r1 = g - g1.astype(F32)
    g2 = r1.astype(BF16)
    g3 = (r1 - g2.astype(F32)).astype(BF16)
    out = jnp.dot(tri, g1, preferred_element_type=F32)
    out += jnp.dot(tri, g2, preferred_element_type=F32)
    out += jnp.dot(tri, g3, preferred_element_type=F32)
    return out


def _gla_chunk(q, fl, v, lb, st, ones, rev):
    cn, sub = GLA_CHUNK, GLA_SUB
    fg = lb + (1.0 - lb) * jax.nn.sigmoid(fl)
    g = jnp.log2(fg)
    kk = 1.0 - fg
    b = _cumsum_time(g, rev)
    b_edge = b[0:1] if rev else b[cn - 1:cn]
    o = lax.dot_general((q * jnp.exp2(b)).astype(BF16), st.astype(BF16),
                        (((1,), (1,)), ((), ())), preferred_element_type=F32)
    kdec = (kk * jnp.exp2(b_edge - b)).astype(BF16)
    st_new = st * jnp.exp2(b_edge) + lax.dot_general(
        v.astype(BF16), kdec, (((0,), (0,)), ((), ())), preferred_element_type=F32)

    vb = v.astype(BF16)
    t_idx = lax.broadcasted_iota(jnp.int32, (SUBLANES, 1), 0)
    pieces = []
    nsub = cn // sub
    for i in range(nsub):
        lo, hi = i * sub, (i + 1) * sub
        bi, qi, ki, vi = b[lo:hi], q[lo:hi], kk[lo:hi], v[lo:hi]
        oi = o[lo:hi]
        if (not rev and i > 0) or (rev and i < nsub - 1):
            if rev:
                ref_b, plo, phi = b[hi:hi + 1], hi, cn
            else:
                ref_b, plo, phi = b[lo - 1:lo], 0, lo
            qd = (qi * jnp.exp2(bi - ref_b)).astype(BF16)
            kd = (kk[plo:phi] * jnp.exp2(ref_b - b[plo:phi])).astype(BF16)
            a = lax.dot_general(qd, kd, (((1,), (1,)), ((), ())), preferred_element_type=F32)
            oi = oi + jnp.dot(a.astype(BF16), vb[plo:phi], preferred_element_type=F32)
        terms, plan = [], []
        for s in range(sub):
            for r0 in range(0, sub, SUBLANES):
                if (r0 > s) if rev else (r0 + SUBLANES - 1 < s):
                    continue
                d = bi[r0:r0 + SUBLANES] - bi[s:s + 1]
                if r0 <= s < r0 + SUBLANES:
                    keep = (t_idx + r0 <= s) if rev else (t_idx + r0 >= s)
                    d = jnp.where(keep, d, NEG_BIG)
                terms.append(qi[r0:r0 + SUBLANES] * jnp.exp2(d) * ki[s:s + 1])
                plan.append((s, r0))
        a = jnp.dot(jnp.concatenate(terms, axis=0).astype(BF16), ones, preferred_element_type=F32)
        halves = {r0: oi[r0:r0 + SUBLANES] for r0 in range(0, sub, SUBLANES)}
        for n, (s, r0) in enumerate(plan):
            halves[r0] = halves[r0] + a[n * SUBLANES:(n + 1) * SUBLANES] * vi[s:s + 1]
        pieces.extend(halves[r0] for r0 in range(0, sub, SUBLANES))
    return jnp.concatenate(pieces, axis=0), st_new


def _hgrn_kernel(qf_ref, ff_ref, vf_ref, qb_ref, fb_ref, vb_ref, lb_ref, of_ref, ob_ref, st_ref):
    @pl.when(pl.program_id(1) == 0)
    def _():
        st_ref[...] = jnp.zeros_like(st_ref)

    ones = jnp.ones((A_DIM, LANES), BF16)
    dirs = ((qf_ref, ff_ref, vf_ref, of_ref, False), (qb_ref, fb_ref, vb_ref, ob_ref, True))
    for di, (q_ref, f_ref, v_ref, o_ref, rev) in enumerate(dirs):
        for h in range(A_HEADS):
            sl = slice(h * A_DIM, (h + 1) * A_DIM)
            o, st = _gla_chunk(q_ref[0, :, sl], f_ref[0, :, sl], v_ref[0, :, sl],
                               lb_ref[di:di + 1, sl], st_ref[di, h], ones, rev)
            o_ref[0, :, sl] = o
            st_ref[di, h] = st


def _hgrn_scan(h, lb):
    b, t, _ = h.shape
    nc = t // GLA_CHUNK
    blk = (1, GLA_CHUNK, A_WIDTH)
    fwd = lambda col: pl.BlockSpec(blk, lambda i, c: (i, c, col))
    bwd = lambda col: pl.BlockSpec(blk, lambda i, c: (i, nc - 1 - c, col))
    out = jax.ShapeDtypeStruct((b, t, A_WIDTH), F32)
    return pl.pallas_call(
        _hgrn_kernel,
        grid=(b, nc),
        in_specs=[fwd(0), fwd(1), fwd(3), bwd(0), bwd(2), bwd(3),
                  pl.BlockSpec((2, A_WIDTH), lambda i, c: (0, 0))],
        out_specs=[pl.BlockSpec(blk, lambda i, c: (i, c, 0)),
                   pl.BlockSpec(blk, lambda i, c: (i, nc - 1 - c, 0))],
        out_shape=[out, out],
        scratch_shapes=[pltpu.VMEM((2, A_HEADS, A_DIM, A_DIM), F32)],
        compiler_params=_params("parallel", "arbitrary"),
    )(h, h, h, h, h, h, lb)


def _swiglu_body(x, wg, wu, wd):
    g = jnp.dot(x, wg, preferred_element_type=F32)
    u = jnp.dot(x, wu, preferred_element_type=F32)
    hid = (g * jax.nn.sigmoid(g) * u).astype(BF16)
    return jnp.dot(hid, wd, preferred_element_type=F32)


def _combine_kernel(y_ref, w_ref, x_ref, xb_ref, wg_ref, wu_ref, wd_ref, g_ref, b_ref, o_ref, ob_ref):
    w = w_ref[...]
    f = _swiglu_body(xb_ref[...], wg_ref[...], wu_ref[...], wd_ref[...])
    for k in range(TOP_K):
        f = f + y_ref[k].astype(F32) * w[:, k:k + 1]
    out = _layer_norm(ALPHA * x_ref[...] + f, g_ref[...], b_ref[...])
    o_ref[...] = out
    ob_ref[...] = out.astype(BF16)


def _moe_combine(y_tok, w_tok, x, xb, wg, wu, wd, g, b, tm=256):
    n, d = x.shape
    f = wg.shape[1]
    tm = _tile(n, tm)
    row = lambda width: pl.BlockSpec((tm, width), lambda i: (i, 0))
    const = lambda shape: pl.BlockSpec(shape, lambda i: (0, 0))
    return pl.pallas_call(
        _combine_kernel,
        grid=(n // tm,),
        in_specs=[pl.BlockSpec((TOP_K, tm, d), lambda i: (0, i, 0)), row(TOP_K), row(d), row(d),
                  const((d, f)), const((d, f)), const((f, d)), const((1, d)), const((1, d))],
        out_specs=[row(d), row(d)],
        out_shape=[jax.ShapeDtypeStruct((n, d), F32), jax.ShapeDtypeStruct((n, d), BF16)],
        compiler_params=_params("parallel"),
    )(y_tok, w_tok, x, xb, wg, wu, wd, g.reshape(1, d), b.reshape(1, d))


def _grouped_kernel(blk_e_ref, nused_ref, x_ref, wg_ref, wu_ref, wd_ref, o_ref):
    @pl.when(pl.program_id(0) < nused_ref[0])
    def _():
        o_ref[...] = _swiglu_body(x_ref[...], wg_ref[0], wu_ref[0], wd_ref[0]).astype(o_ref.dtype)


def _grouped_experts(x_sorted, blk_e, nused, wg, wu, wd):
    p, d = x_sorted.shape
    f = wg.shape[2]
    nblk = p // MOE_ROWS
    row = lambda i, be, nu: (jnp.minimum(i, nu[0] - 1), 0)
    exp = lambda i, be, nu: (be[jnp.minimum(i, nu[0] - 1)], 0, 0)
    return pl.pallas_call(
        _grouped_kernel,
        grid_spec=pltpu.PrefetchScalarGridSpec(
            num_scalar_prefetch=2,
            grid=(nblk,),
            in_specs=[pl.BlockSpec((MOE_ROWS, d), row),
                      pl.BlockSpec((1, d, f), exp),
                      pl.BlockSpec((1, d, f), exp),
                      pl.BlockSpec((1, f, d), exp)],
            out_specs=pl.BlockSpec((MOE_ROWS, d), row)),
        out_shape=jax.ShapeDtypeStruct((p, d), BF16),
        compiler_params=_params("arbitrary"),
    )(blk_e, nused, x_sorted, wg, wu, wd)


def _rms_norm(x, w):
    return x * lax.rsqrt(jnp.mean(x * x, axis=-1, keepdims=True) + RMS_EPS) * w


def _layer_norm(x, g, b):
    mu = jnp.mean(x, axis=-1, keepdims=True)
    xc = x - mu
    var = jnp.mean(xc * xc, axis=-1, keepdims=True)
    return xc * lax.rsqrt(var + LN_EPS) * g + b


def _rope_tables(t, rot_dim):
    row = (jnp.arange(t) // GRID_W).astype(F32)
    col = (jnp.arange(t) % GRID_W).astype(F32)
    half = rot_dim // 2
    inv = ROPE_THETA ** (-jnp.arange(0, half, 2, dtype=F32) / half)
    ang_r = row[:, None] * inv[None, :]
    ang_c = col[:, None] * inv[None, :]
    return jnp.cos(ang_r), jnp.sin(ang_r), jnp.cos(ang_c), jnp.sin(ang_c)


def _axial_rope(x, tables):
    cr, sr, cc, sc = tables
    extra = x.ndim - 3
    ex = lambda a: a.reshape(a.shape[0], *([1] * extra), a.shape[1])

    def rot(y, cos, sin):
        y1, y2 = jnp.split(y, 2, axis=-1)
        return jnp.concatenate([y1 * cos - y2 * sin, y1 * sin + y2 * cos], axis=-1)

    half = x.shape[-1] // 2
    return jnp.concatenate([rot(x[..., :half], ex(cr), ex(sr)),
                            rot(x[..., half:], ex(cc), ex(sc))], axis=-1)


def _even_mixer(x, shapes, w_in, lb, hgrn_norm_w, q_norm_w, k_norm_w):
    h = _matmul(x, w_in.astype(BF16), F32)
    mixes = []
    off = 0
    for (b, t) in shapes:
        ht = h[off:off + b * t].reshape(b, t, -1)
        off += b * t
        o_f, o_b = _hgrn_scan(ht, lb.reshape(2, A_WIDTH))
        o = (o_f + o_b).reshape(b, t, A_HEADS, A_DIM)
        gate = ht[..., 4 * A_WIDTH:5 * A_WIDTH]
        a_out = (_rms_norm(o, hgrn_norm_w) * jax.nn.silu(gate.reshape(b, t, A_HEADS, A_DIM)))
        a_out = a_out.reshape(b, t, A_WIDTH)

        tables = _rope_tables(t, B_HEAD_DIM)
        c0 = 5 * A_WIDTH
        bq = ht[..., c0:c0 + 512].reshape(b, t, B_HEADS, B_HEAD_DIM)
        bk = ht[..., c0 + 512:c0 + 768].reshape(b, t, B_KV_HEADS, B_HEAD_DIM)
        bv = ht[..., c0 + 768:c0 + 1024]
        bq = _axial_rope(_rms_norm(bq, q_norm_w), tables) * (B_HEAD_DIM ** -0.5 * LOG2_E)
        bk = _axial_rope(_rms_norm(bk, k_norm_w), tables)
        b_out = _gqa_attention(bq.reshape(b, t, -1).astype(BF16), bk.reshape(b, t, -1).astype(BF16),
                               bv.astype(BF16))
        mixes.append(jnp.concatenate([a_out.astype(BF16), b_out], axis=-1).reshape(b * t, -1))
    return jnp.concatenate(mixes, axis=0)


def _mla_mixer(x, shapes, w_in, q_a_norm_w, w_q_b, kv_a_norm_w, w_kv_b):
    hc = _matmul(x, w_in.astype(BF16), F32)
    cq = _rms_norm(hc[:, :C_Q_LORA], q_a_norm_w)
    ckv = _rms_norm(hc[:, C_Q_LORA:C_Q_LORA + C_KV_LORA], kv_a_norm_w)
    k_rope = hc[:, C_Q_LORA + C_KV_LORA:]
    wkv = w_kv_b.reshape(C_KV_LORA, C_HEADS, C_NOPE + C_V)
    wkv = jnp.concatenate([wkv[:, :, :C_NOPE].reshape(C_KV_LORA, -1),
                           wkv[:, :, C_NOPE:].reshape(C_KV_LORA, -1)], axis=1)
    q = _matmul(cq, w_q_b.astype(BF16), F32)
    kv = _matmul(ckv, wkv.astype(BF16), BF16)
    scale = (C_NOPE + C_ROPE) ** -0.5 * LOG2_E
    outs = []
    off = 0
    for (b, t) in shapes:
        sl = slice(off, off + b * t)
        off += b * t
        tables = _rope_tables(t, C_ROPE)
        qt = q[sl].reshape(b, t, C_HEADS, C_NOPE + C_ROPE)
        q_rope = _axial_rope(qt[..., C_NOPE:], tables)
        qc = (jnp.concatenate([qt[..., :C_NOPE], q_rope], axis=-1) * scale).astype(BF16)
        qc = qc.transpose(0, 2, 1, 3)
        kr = _axial_rope(k_rope[sl].reshape(b, t, C_ROPE), tables).astype(BF16)
        k_nope = kv[sl, :C_HEADS * C_NOPE].reshape(b, t, C_HEADS, C_NOPE)
        kc = jnp.concatenate([k_nope, jnp.broadcast_to(kr[:, :, None, :], (b, t, C_HEADS, C_ROPE))],
                             axis=-1).transpose(0, 2, 1, 3)
        v = kv[sl, C_HEADS * C_NOPE:].reshape(b, t, C_HEADS * C_V)
        outs.append(_mla_attention(qc, kc, v).reshape(b * t, -1))
    return jnp.concatenate(outs, axis=0)


def _moe_block(x, xb, w_router, router_bias, w_gate, w_up, w_down, ws_gate, ws_up, ws_down, ln_g, ln_b):
    n, d = x.shape
    tm = _tile(n, 512)
    top_e, top_w, tile_cnt = _route(x, w_router, router_bias, tm)
    tile_cnt = tile_cnt.astype(jnp.int32)
    counts = jnp.sum(tile_cnt, axis=0)
    padded = (counts + MOE_ROWS - 1) // MOE_ROWS * MOE_ROWS
    pend = jnp.cumsum(padded)
    pstart = pend - padded
    tile_base = pstart[None, :] + jnp.cumsum(tile_cnt, axis=0) - tile_cnt
    dest = _dest_rows(top_e, tile_base, tm)
    nblk = (n * TOP_K) // MOE_ROWS + N_EXPERTS
    blk_e = jnp.minimum(jnp.searchsorted(pend, jnp.arange(nblk) * MOE_ROWS, side='right'),
                        N_EXPERTS - 1).astype(jnp.int32)
    nused = (pend[-1] // MOE_ROWS).astype(jnp.int32).reshape(1)
    tok = jnp.broadcast_to(jnp.arange(n, dtype=jnp.int32)[None, :], (TOP_K, n))
    pad_tok = jnp.arange(nblk * MOE_ROWS, dtype=jnp.int32) % n
    slot_tok = pad_tok.at[dest.reshape(-1)].set(tok.reshape(-1), unique_indices=True,
                                                mode="promise_in_bounds")
    x_sorted = xb.at[slot_tok].get(mode="promise_in_bounds")
    y_sorted = _grouped_experts(x_sorted, blk_e, nused, w_gate.astype(BF16), w_up.astype(BF16),
                                w_down.astype(BF16))
    y_tok = y_sorted.at[dest.reshape(-1)].get(mode="promise_in_bounds").reshape(TOP_K, n, d)
    return _moe_combine(y_tok, top_w.T, x, xb, ws_gate.astype(BF16), ws_up.astype(BF16),
                        ws_down.astype(BF16), ln_g, ln_b)


def _trunks(xs, params):
    (w_in_even, lb_logits, hgrn_norm_w, q_norm_w, k_norm_w, w_out_even,
     w_in_odd, q_a_norm_w, w_q_b, kv_a_norm_w, w_kv_b, w_out_odd,
     w_router, router_bias, w_gate, w_up, w_down, ws_gate, ws_up, ws_down,
     ln_mix_g, ln_mix_b, ln_ffn_g, ln_ffn_b) = params
    shapes = [(x.shape[0], x.shape[1]) for x in xs]
    d = xs[0].shape[-1]
    x = jnp.concatenate([a.reshape(-1, d) for a in xs], axis=0)
    xb = x.astype(BF16)
    lb_all = jnp.cumsum(jax.nn.softmax(lb_logits.astype(F32), axis=0), axis=0)
    for l in range(DEPTH):
        j = l // 2
        if l % 2 == 0:
            mix = _even_mixer(xb, shapes, w_in_even[j], lb_all[j], hgrn_norm_w[j], q_norm_w[j],
                              k_norm_w[j])
            w_out = w_out_even[j]
        else:
            mix = _mla_mixer(xb, shapes, w_in_odd[j], q_a_norm_w[j], w_q_b[j], kv_a_norm_w[j],
                             w_kv_b[j])
            w_out = w_out_odd[j]
        x, xb = _matmul_residual_ln(mix, w_out.astype(BF16), x, ln_mix_g[l], ln_mix_b[l])
        x, xb = _moe_block(x, xb, w_router[l], router_bias[l], w_gate[l], w_up[l], w_down[l],
                           ws_gate[l], ws_up[l], ws_down[l], ln_ffn_g[l], ln_ffn_b[l])
    outs = []
    off = 0
    for (b, t) in shapes:
        outs.append(x[off:off + b * t].reshape(b, t, d))
        off += b * t
    return tuple(outs)


def kernel(x_prompt, x_sample, w_in_even, lb_logits, hgrn_norm_w, q_norm_w, k_norm_w, w_out_even,
           w_in_odd, q_a_norm_w, w_q_b, kv_a_norm_w, w_kv_b, w_out_odd,
           w_router, router_bias, w_gate, w_up, w_down, ws_gate, ws_up, ws_down,
           ln_mix_g, ln_mix_b, ln_ffn_g, ln_ffn_b):
    params = (w_in_even, lb_logits, hgrn_norm_w, q_norm_w, k_norm_w, w_out_even,
              w_in_odd, q_a_norm_w, w_q_b, kv_a_norm_w, w_kv_b, w_out_odd,
              w_router, router_bias, w_gate, w_up, w_down, ws_gate, ws_up, ws_down,
              ln_mix_g, ln_mix_b, ln_ffn_g, ln_ffn_b)
    return _trunks([x_prompt, x_sample], params)
```

```python
import functools

import jax
import jax.numpy as jnp
from jax import lax
from jax.experimental import pallas as pl
from jax.experimental.pallas import tpu as pltpu

F32 = jnp.float32
BF16 = jnp.bfloat16

GRID_W = 64
ROPE_THETA = 10000.0
RMS_EPS = 1e-6
LN_EPS = 1e-5
DEPTH = 2
ALPHA = (2 * DEPTH) ** 0.25

A_HEADS = 4
A_DIM = 128
A_WIDTH = A_HEADS * A_DIM
A_COLS = 5 * A_WIDTH
B_HEADS = 4
B_KV_HEADS = 2
B_GROUP = B_HEADS // B_KV_HEADS
B_HEAD_DIM = 128
B_WIDTH = B_HEADS * B_HEAD_DIM
B_KVW = B_KV_HEADS * B_HEAD_DIM
C_HEADS = 8
C_NOPE = 128
C_ROPE = 64
C_V = 128
C_Q_LORA = 384
C_KV_LORA = 256
C_QK_PAD = 256
N_EXPERTS = 256
TOP_K = 8
N_GROUPS = 8
TOPK_GROUPS = 4
ROUTED_SCALE = 2.5

V7X_VMEM_LIMIT_BYTES = 56 * 1024 * 1024
LANES = 128
SUBLANES = 8

TOKEN_TILE = 512
GLA_CHUNK = 64
GLA_SUB = 16
MOE_ROWS = 512
NEG_BIG = -1e30
LOG2_E = 1.4426950408889634

_NT = (((1,), (1,)), ((), ()))


def _params(*semantics):
    return pltpu.CompilerParams(dimension_semantics=semantics,
                                vmem_limit_bytes=V7X_VMEM_LIMIT_BYTES)


def _tile(n, want):
    t = min(n, want)
    assert n % t == 0, (n, t)
    return t


def _rows(tm, width):
    return pl.BlockSpec((tm, width), lambda i: (i, 0))


def _const(shape):
    return pl.BlockSpec(shape, lambda i: (0,) * len(shape))


def _layer_norm(x, g, b):
    mu = jnp.mean(x, axis=-1, keepdims=True)
    xc = x - mu
    var = jnp.mean(xc * xc, axis=-1, keepdims=True)
    return xc * lax.rsqrt(var + LN_EPS) * g + b


def _inv_rms(x):
    return lax.rsqrt(jnp.mean(x * x, axis=-1, keepdims=True) + RMS_EPS)


def _rope_swap(n):
    q = n // 4
    a = jnp.arange(q)
    return jnp.concatenate([a + q, a, a + 3 * q, a + 2 * q])


def _rope_cs(t, rot_dim):
    row = (jnp.arange(t) // GRID_W).astype(F32)
    col = (jnp.arange(t) % GRID_W).astype(F32)
    half = rot_dim // 2
    inv = ROPE_THETA ** (-jnp.arange(0, half, 2, dtype=F32) / half)
    ar = row[:, None] * inv[None, :]
    ac = col[:, None] * inv[None, :]
    c = jnp.concatenate([jnp.cos(ar), jnp.cos(ar), jnp.cos(ac), jnp.cos(ac)], axis=1)
    s = jnp.concatenate([-jnp.sin(ar), jnp.sin(ar), -jnp.sin(ac), jnp.sin(ac)], axis=1)
    pad = ((0, 0), (0, LANES - rot_dim))
    return jnp.pad(c, pad), jnp.pad(s, pad)


def _pos_block(shapes, tm):
    groups = []
    first = 0
    for (b, t) in shapes:
        groups.append((first, t // tm))
        first += b * t // tm

    def index_map(i):
        first_tile, per_seq = groups[0]
        blk = (i - first_tile) % per_seq
        for first_tile, per_seq in groups[1:]:
            blk = jnp.where(i >= first_tile, (i - first_tile) % per_seq, blk)
        return (blk, 0)

    return index_map


def _even_prep_kernel(x_ref, w_ref, qw_ref, kw_ref, c_ref, s_ref, ha_ref, q_ref, k_ref, v_ref):
    xb = x_ref[...]
    for j in range(0, A_COLS, A_WIDTH):
        ha_ref[:, j:j + A_WIDTH] = jnp.dot(xb, w_ref[:, j:j + A_WIDTH], preferred_element_type=F32)
    c = c_ref[...]
    s = s_ref[...]
    q0, k0, v0 = A_COLS, A_COLS + B_WIDTH, A_COLS + B_WIDTH + B_KVW
    qs0 = v0 + B_KVW
    ks0 = qs0 + B_WIDTH
    d = B_HEAD_DIM

    def normed_rope(col, col_sw, heads, w2_ref, scale, o_ref):
        y = jnp.dot(xb, w_ref[:, col:col + heads * d], preferred_element_type=F32)
        ysw = jnp.dot(xb, w_ref[:, col_sw:col_sw + heads * d], preferred_element_type=F32)
        for h in range(heads):
            sl = slice(h * d, (h + 1) * d)
            yh = y[:, sl]
            rot = yh * (w2_ref[0:1] * c) + ysw[:, sl] * (w2_ref[1:2] * s)
            o_ref[:, sl] = (rot * (_inv_rms(yh) * scale)).astype(BF16)

    normed_rope(q0, qs0, B_HEADS, qw_ref, B_HEAD_DIM ** -0.5 * LOG2_E, q_ref)
    normed_rope(k0, ks0, B_KV_HEADS, kw_ref, 1.0, k_ref)
    v_ref[...] = jnp.dot(xb, w_ref[:, v0:v0 + B_KVW], preferred_element_type=F32).astype(BF16)


def _even_prep(xb, shapes, w_in, q_norm_w, k_norm_w, tables):
    n, dm = xb.shape
    tm = TOKEN_TILE
    swap = _rope_swap(B_HEAD_DIM)
    q0, k0 = A_COLS, A_COLS + B_WIDTH
    wq = w_in[:, q0:q0 + B_WIDTH].reshape(dm, B_HEADS, B_HEAD_DIM)[:, :, swap].reshape(dm, B_WIDTH)
    wk = w_in[:, k0:k0 + B_KVW].reshape(dm, B_KV_HEADS, B_HEAD_DIM)[:, :, swap].reshape(dm, B_KVW)
    w_ext = jnp.concatenate([w_in, wq, wk], axis=1).astype(BF16)
    qw = jnp.stack([q_norm_w, q_norm_w[swap]]).astype(F32)
    kw = jnp.stack([k_norm_w, k_norm_w[swap]]).astype(F32)
    pos = pl.BlockSpec((tm, LANES), _pos_block(shapes, tm))
    return pl.pallas_call(
        _even_prep_kernel,
        grid=(n // tm,),
        in_specs=[_rows(tm, dm), _const(w_ext.shape), _const((2, B_HEAD_DIM)), _const((2, B_HEAD_DIM)),
                  pos, pos],
        out_specs=[_rows(tm, A_COLS), _rows(tm, B_WIDTH), _rows(tm, B_KVW), _rows(tm, B_KVW)],
        out_shape=[jax.ShapeDtypeStruct((n, A_COLS), F32), jax.ShapeDtypeStruct((n, B_WIDTH), BF16),
                   jax.ShapeDtypeStruct((n, B_KVW), BF16), jax.ShapeDtypeStruct((n, B_KVW), BF16)],
        compiler_params=_params("parallel"),
    )(xb, w_ext, qw, kw, tables[0], tables[1])


def _even_out_kernel(of_ref, ob_ref, g_ref, att_ref, nw_ref, w_ref, r_ref, lg_ref, lb_ref, o_ref, ob16_ref):
    acc = ALPHA * r_ref[...] + jnp.dot(att_ref[...], w_ref[A_WIDTH:, :], preferred_element_type=F32)
    nw = nw_ref[...]
    parts = []
    for h in range(A_HEADS):
        sl = slice(h * A_DIM, (h + 1) * A_DIM)
        o = of_ref[:, sl] + ob_ref[:, sl]
        g = g_ref[:, sl]
        parts.append((o * _inv_rms(o) * nw * (g * jax.nn.sigmoid(g))).astype(BF16))
    acc += jnp.dot(jnp.concatenate(parts, axis=1), w_ref[:A_WIDTH, :], preferred_element_type=F32)
    out = _layer_norm(acc, lg_ref[...], lb_ref[...])
    o_ref[...] = out
    ob16_ref[...] = out.astype(BF16)


def _even_out(o_f, o_b, ha, att, hgrn_norm_w, w_out, x, ln_g, ln_b):
    n, d = x.shape
    tm = TOKEN_TILE
    gate = pl.BlockSpec((tm, A_WIDTH), lambda i: (i, 4))
    return pl.pallas_call(
        _even_out_kernel,
        grid=(n // tm,),
        in_specs=[_rows(tm, A_WIDTH), _rows(tm, A_WIDTH), gate, _rows(tm, B_WIDTH), _const((1, A_DIM)),
                  _const(w_out.shape), _rows(tm, d), _const((1, d)), _const((1, d))],
        out_specs=[_rows(tm, d), _rows(tm, d)],
        out_shape=[jax.ShapeDtypeStruct((n, d), F32), jax.ShapeDtypeStruct((n, d), BF16)],
        compiler_params=_params("parallel"),
    )(o_f, o_b, ha, att, hgrn_norm_w.reshape(1, A_DIM).astype(F32), w_out.astype(BF16), x,
      ln_g.reshape(1, d), ln_b.reshape(1, d))


def _mla_prep_kernel(x_ref, win_ref, qn_ref, kn_ref, wq_ref, wkv_ref, c_ref, s_ref, q_ref, k_ref, v_ref):
    xb = x_ref[...]
    c = c_ref[...]
    s = s_ref[...]
    hc = jnp.dot(xb, win_ref[...], preferred_element_type=F32)
    cq = hc[:, :C_Q_LORA]
    cq = (cq * _inv_rms(cq) * qn_ref[...]).astype(BF16)
    ckv = hc[:, C_Q_LORA:C_Q_LORA + C_KV_LORA]
    ckv = (ckv * _inv_rms(ckv) * kn_ref[...]).astype(BF16)
    kr0 = C_Q_LORA + C_KV_LORA
    k_rope = (hc[:, kr0:kr0 + LANES] * c + hc[:, kr0 + LANES:kr0 + 2 * LANES] * s).astype(BF16)
    scale = (C_NOPE + C_ROPE) ** -0.5 * LOG2_E
    hw = C_HEADS * C_NOPE
    q_nope = jnp.dot(cq, wq_ref[:, :hw], preferred_element_type=F32)
    q_r = jnp.dot(cq, wq_ref[:, hw:2 * hw], preferred_element_type=F32)
    q_rs = jnp.dot(cq, wq_ref[:, 2 * hw:], preferred_element_type=F32)
    k_nope = jnp.dot(ckv, wkv_ref[:, :hw], preferred_element_type=F32)
    v_ref[...] = jnp.dot(ckv, wkv_ref[:, hw:], preferred_element_type=F32).astype(BF16)
    for h in range(C_HEADS):
        sl = slice(h * LANES, (h + 1) * LANES)
        lo = h * C_QK_PAD
        q_ref[:, lo:lo + LANES] = (q_nope[:, sl] * scale).astype(BF16)
        q_ref[:, lo + LANES:lo + 2 * LANES] = ((q_r[:, sl] * c + q_rs[:, sl] * s) * scale).astype(BF16)
        k_ref[:, lo:lo + LANES] = k_nope[:, sl].astype(BF16)
        k_ref[:, lo + LANES:lo + 2 * LANES] = k_rope


def _mla_prep(xb, shapes, w_in, q_a_norm_w, w_q_b, kv_a_norm_w, w_kv_b, tables):
    n, dm = xb.shape
    tm = TOKEN_TILE
    swap = _rope_swap(C_ROPE)
    lane_pad = lambda w: jnp.pad(w, ((0, 0),) * (w.ndim - 1) + ((0, LANES - C_ROPE),))
    kr0 = C_Q_LORA + C_KV_LORA
    w_kr = w_in[:, kr0:]
    win = jnp.concatenate([w_in[:, :kr0], lane_pad(w_kr), lane_pad(w_kr[:, swap])], axis=1).astype(BF16)
    wq = w_q_b.reshape(C_Q_LORA, C_HEADS, C_NOPE + C_ROPE)
    wq_r = wq[:, :, C_NOPE:]
    hw = C_HEADS * LANES
    wq_all = jnp.concatenate([wq[:, :, :C_NOPE].reshape(C_Q_LORA, hw),
                              lane_pad(wq_r).reshape(C_Q_LORA, hw),
                              lane_pad(wq_r[:, :, swap]).reshape(C_Q_LORA, hw)], axis=1).astype(BF16)
    wkv = w_kv_b.reshape(C_KV_LORA, C_HEADS, C_NOPE + C_V)
    wkv_all = jnp.concatenate([wkv[:, :, :C_NOPE].reshape(C_KV_LORA, hw),
                               wkv[:, :, C_NOPE:].reshape(C_KV_LORA, hw)], axis=1).astype(BF16)
    pos = pl.BlockSpec((tm, LANES), _pos_block(shapes, tm))
    qk = jax.ShapeDtypeStruct((n, C_HEADS * C_QK_PAD), BF16)
    return pl.pallas_call(
        _mla_prep_kernel,
        grid=(n // tm,),
        in_specs=[_rows(tm, dm), _const(win.shape), _const((1, C_Q_LORA)), _const((1, C_KV_LORA)),
                  _const(wq_all.shape), _const(wkv_all.shape), pos, pos],
        out_specs=[_rows(tm, C_HEADS * C_QK_PAD), _rows(tm, C_HEADS * C_QK_PAD), _rows(tm, C_HEADS * C_V)],
        out_shape=[qk, qk, jax.ShapeDtypeStruct((n, C_HEADS * C_V), BF16)],
        compiler_params=_params("parallel"),
    )(xb, win, q_a_norm_w.reshape(1, -1).astype(F32), kv_a_norm_w.reshape(1, -1).astype(F32),
      wq_all, wkv_all, tables[0], tables[1])


def _mm_ln_kernel(x_ref, w_ref, r_ref, g_ref, b_ref, o_ref, ob_ref):
    y = ALPHA * r_ref[...] + jnp.dot(x_ref[...], w_ref[...], preferred_element_type=F32)
    out = _layer_norm(y, g_ref[...], b_ref[...])
    o_ref[...] = out
    ob_ref[...] = out.astype(BF16)


def _matmul_residual_ln(x, w, resid, g, b):
    m, k = x.shape
    d = w.shape[1]
    tm = TOKEN_TILE
    return pl.pallas_call(
        _mm_ln_kernel,
        grid=(m // tm,),
        in_specs=[_rows(tm, k), _const((k, d)), _rows(tm, d), _const((1, d)), _const((1, d))],
        out_specs=[_rows(tm, d), _rows(tm, d)],
        out_shape=[jax.ShapeDtypeStruct((m, d), F32), jax.ShapeDtypeStruct((m, d), BF16)],
        compiler_params=_params("parallel"),
    )(x, w, resid, g.reshape(1, d), b.reshape(1, d))


def _row_iota(rows, cols):
    return lax.broadcasted_iota(jnp.int32, (rows, cols), 0).astype(F32)


def _first_max(cur, iota, n):
    m = jnp.max(cur, axis=0, keepdims=True)
    idx = jnp.min(jnp.where(cur == m, iota, float(n)), axis=0, keepdims=True)
    return m, idx


def _route_kernel(x_ref, wh_ref, wl_ref, bias_ref, e_ref, w_ref, cnt_ref):
    x = x_ref[...]
    tm = x.shape[0]
    xh = x.astype(BF16)
    xl = (x - xh.astype(F32)).astype(BF16)
    wh = wh_ref[...]
    logits = lax.dot_general(wh, xh, _NT, preferred_element_type=F32)
    logits += lax.dot_general(wh, xl, _NT, preferred_element_type=F32)
    logits += lax.dot_general(wl_ref[...], xh, _NT, preferred_element_type=F32)
    scores = jax.nn.sigmoid(logits)
    choice = scores + bias_ref[...]
    gsz = N_EXPERTS // N_GROUPS
    sub_iota = _row_iota(gsz, tm)
    group_rows = []
    for g in range(N_GROUPS):
        grp = choice[g * gsz:(g + 1) * gsz]
        m1, first = _first_max(grp, sub_iota, gsz)
        m2 = jnp.max(jnp.where(sub_iota == first, -jnp.inf, grp), axis=0, keepdims=True)
        group_rows.append(m1 + m2)
    cur = jnp.concatenate(group_rows, axis=0)
    g_iota = _row_iota(N_GROUPS, tm)
    sel = jnp.zeros((N_GROUPS, tm), F32)
    for _ in range(TOPK_GROUPS):
        _, gi = _first_max(cur, g_iota, N_GROUPS)
        pick = g_iota == gi
        sel = jnp.where(pick, 1.0, sel)
        cur = jnp.where(pick, -jnp.inf, cur)
    masked = jnp.concatenate(
        [jnp.where(sel[g:g + 1] > 0.5, choice[g * gsz:(g + 1) * gsz], -jnp.inf)
         for g in range(N_GROUPS)], axis=0)
    e_iota = _row_iota(N_EXPERTS, tm)
    cur = masked
    ids, vals = [], []
    for _ in range(TOP_K):
        _, ei = _first_max(cur, e_iota, N_EXPERTS)
        pick = e_iota == ei
        vals.append(jnp.sum(jnp.where(pick, scores, 0.0), axis=0, keepdims=True))
        ids.append(ei)
        cur = jnp.where(pick, -jnp.inf, cur)
    member = jnp.where((masked > -jnp.inf) & (cur == -jnp.inf), 1.0, 0.0).astype(BF16)
    s = jnp.concatenate(vals, axis=0)
    e_ref[...] = jnp.concatenate(ids, axis=0).astype(jnp.int32)
    w_ref[...] = s / (jnp.sum(s, axis=0, keepdims=True) + 1e-20) * ROUTED_SCALE
    cnt_ref[...] = lax.dot_general(jnp.ones((8, tm), BF16), member, _NT, preferred_element_type=F32)


def _route(x, w_router, router_bias, tm):
    n, d = x.shape
    wt = w_router.T
    wh = wt.astype(BF16)
    wl = (wt - wh.astype(F32)).astype(BF16)
    e, w, cnt = pl.pallas_call(
        _route_kernel,
        grid=(n // tm,),
        in_specs=[_rows(tm, d), _const((N_EXPERTS, d)), _const((N_EXPERTS, d)), _const((N_EXPERTS, 1))],
        out_specs=[pl.BlockSpec((TOP_K, tm), lambda i: (0, i)),
                   pl.BlockSpec((TOP_K, tm), lambda i: (0, i)),
                   pl.BlockSpec((None, 8, N_EXPERTS), lambda i: (i, 0, 0))],
        out_shape=[jax.ShapeDtypeStruct((TOP_K, n), jnp.int32),
                   jax.ShapeDtypeStruct((TOP_K, n), F32),
                   jax.ShapeDtypeStruct((n // tm, 8, N_EXPERTS), F32)],
        compiler_params=_params("parallel"),
    )(x, wh, wl, router_bias.reshape(N_EXPERTS, 1).astype(F32))
    return e, w, cnt[:, 0, :]


def _dest_kernel(e_ref, base_ref, d_ref):
    e = e_ref[...].astype(F32)
    tm = e.shape[1]
    e_iota = _row_iota(N_EXPERTS, tm)
    picks = [e_iota == e[k:k + 1] for k in range(TOP_K)]
    member = jnp.zeros((N_EXPERTS, tm), F32)
    for p in picks:
        member = jnp.where(p, 1.0, member)
    r = lax.broadcasted_iota(jnp.int32, (tm, tm), 0)
    c = lax.broadcasted_iota(jnp.int32, (tm, tm), 1)
    upper = jnp.where(r <= c, 1.0, 0.0).astype(BF16)
    prefix = jnp.dot(member.astype(BF16), upper, preferred_element_type=F32)
    val = prefix + (base_ref[...] - 1.0)
    rows = [jnp.sum(jnp.where(p, val, 0.0), axis=0, keepdims=True) for p in picks]
    d_ref[...] = jnp.concatenate(rows, axis=0).astype(jnp.int32)


def _dest_rows(top_e, tile_base, tm):
    k, n = top_e.shape
    return pl.pallas_call(
        _dest_kernel,
        grid=(n // tm,),
        in_specs=[pl.BlockSpec((k, tm), lambda i: (0, i)),
                  pl.BlockSpec((None, N_EXPERTS, 1), lambda i: (i, 0, 0))],
        out_specs=pl.BlockSpec((k, tm), lambda i: (0, i)),
        out_shape=jax.ShapeDtypeStruct((k, n), jnp.int32),
        compiler_params=_params("parallel"),
    )(top_e, tile_base.astype(F32)[:, :, None])


def _flash_kernel(q_ref, k_ref, v_ref, *rest, groups, dq, dv, tk):
    o_ref = rest[-1]
    tq = q_ref.shape[0]
    t = k_ref.shape[0]
    qs = [q_ref[:, g * dq:(g + 1) * dq] for g in range(groups)]

    def body(c, carry):
        start = pl.multiple_of(c * tk, tk)
        k = k_ref[pl.ds(start, tk), :]
        v = v_ref[pl.ds(start, tk), :]
        out = []
        for g in range(groups):
            m, l, acc = carry[g]
            s = lax.dot_general(qs[g], k, _NT, preferred_element_type=F32)
            m_new = jnp.maximum(m, jnp.max(s, axis=-1, keepdims=True))
            alpha = jnp.exp2(m - m_new)
            p = jnp.exp2(s - m_new)
            l = alpha * l + jnp.sum(p, axis=-1, keepdims=True)
            acc = alpha * acc + jnp.dot(p.astype(BF16), v, preferred_element_type=F32)
            out.append((m_new, l, acc))
        return tuple(out)

    init = tuple((jnp.full((tq, 1), NEG_BIG, F32), jnp.zeros((tq, 1), F32),
                  jnp.zeros((tq, dv), F32)) for _ in range(groups))
    res = lax.fori_loop(0, t // tk, body, init, unroll=2)
    for g in range(groups):
        _, l, acc = res[g]
        o_ref[:, g * dv:(g + 1) * dv] = (acc / l).astype(o_ref.dtype)


def _seq_view(a, t):
    return a.reshape(a.shape[0] // t, t, a.shape[1])


def _carry_output(prev, n_in):
    if prev is None:
        return [], [], {}
    prev = list(prev)
    return prev, [pl.BlockSpec(memory_space=pl.ANY)] * len(prev), {n_in + k: k for k in range(len(prev))}


def _attention(q, k, v, seqs, t, first_seq, prev, *, kv_heads, groups, dq, dv, tq=512, tk=1024):
    n = q.shape[0]
    tq = _tile(t, tq)
    tk = _tile(t, tk)
    kern = functools.partial(_flash_kernel, groups=groups, dq=dq, dv=dv, tk=tk)
    extra, extra_specs, aliases = _carry_output(None if prev is None else [_seq_view(prev, t)], 3)
    out = pl.pallas_call(
        kern,
        grid=(seqs, kv_heads, t // tq),
        in_specs=[pl.BlockSpec((None, tq, groups * dq), lambda i, h, j: (i + first_seq, j, h)),
                  pl.BlockSpec((None, t, dq), lambda i, h, j: (i + first_seq, 0, h)),
                  pl.BlockSpec((None, t, dv), lambda i, h, j: (i + first_seq, 0, h))] + extra_specs,
        out_specs=pl.BlockSpec((None, tq, groups * dv), lambda i, h, j: (i + first_seq, j, h)),
        out_shape=jax.ShapeDtypeStruct((n // t, t, kv_heads * groups * dv), BF16),
        input_output_aliases=aliases,
        compiler_params=_params("parallel", "parallel", "arbitrary"),
    )(_seq_view(q, t), _seq_view(k, t), _seq_view(v, t), *extra)
    return out.reshape(n, -1)


def _cumsum_time(g, rev):
    c = g.shape[0]
    row = lax.broadcasted_iota(jnp.int32, (c, c), 0)
    col = lax.broadcasted_iota(jnp.int32, (c, c), 1)
    tri = jnp.where((col >= row) if rev else (col <= row), 1.0, 0.0).astype(BF16)
    g1 = g.astype(BF16)
    r1 = g - g1.astype(F32)
    g2 = r1.astype(BF16)
    g3 = (r1 - g2.astype(F32)).astype(BF16)
    out = jnp.dot(tri, g1, preferred_element_type=F32)
    out += jnp.dot(tri, g2, preferred_element_type=F32)
    out += jnp.dot(tri, g3, preferred_element_type=F32)
    return out


def _gla_gates(fl, lb, rev):
    fg = lb + (1.0 - lb) * jax.nn.sigmoid(fl)
    return 1.0 - fg, _cumsum_time(jnp.log2(fg), rev)


def _gla_state(q, v, kk, b, st, rev):
    b_edge = b[0:1] if rev else b[GLA_CHUNK - 1:GLA_CHUNK]
    o = lax.dot_general((q * jnp.exp2(b)).astype(BF16), st.astype(BF16), _NT,
                        preferred_element_type=F32)
    kdec = (kk * jnp.exp2(b_edge - b)).astype(BF16)
    st_new = st * jnp.exp2(b_edge) + lax.dot_general(
        v.astype(BF16), kdec, (((0,), (0,)), ((), ())), preferred_element_type=F32)
    return o, st_new


def _gla_earlier_blocks(i, q, v, kk, b, rev):
    cn, sub = GLA_CHUNK, GLA_SUB
    lo, hi = i * sub, (i + 1) * sub
    if rev:
        if hi == cn:
            return None
        ref_b, plo, phi = b[hi:hi + 1], hi, cn
    else:
        if lo == 0:
            return None
        ref_b, plo, phi = b[lo - 1:lo], 0, lo
    qd = (q[lo:hi] * jnp.exp2(b[lo:hi] - ref_b)).astype(BF16)
    kd = (kk[plo:phi] * jnp.exp2(ref_b - b[plo:phi])).astype(BF16)
    a = lax.dot_general(qd, kd, _NT, preferred_element_type=F32)
    return jnp.dot(a.astype(BF16), v[plo:phi].astype(BF16), preferred_element_type=F32)


def _gla_diagonal(i, q, v, kk, b, ones, rev):
    sub = GLA_SUB
    lo = i * sub
    bi, qi, ki, vi = b[lo:lo + sub], q[lo:lo + sub], kk[lo:lo + sub], v[lo:lo + sub]
    t_idx = lax.broadcasted_iota(jnp.int32, (SUBLANES, 1), 0)
    terms, plan = [], []
    for s in range(sub):
        for r0 in range(0, sub, SUBLANES):
            if (r0 > s) if rev else (r0 + SUBLANES - 1 < s):
                continue
            d = bi[r0:r0 + SUBLANES] - bi[s:s + 1]
            if r0 <= s < r0 + SUBLANES:
                keep = (t_idx + r0 <= s) if rev else (t_idx + r0 >= s)
                d = jnp.where(keep, d, NEG_BIG)
            terms.append(qi[r0:r0 + SUBLANES] * jnp.exp2(d) * ki[s:s + 1])
            plan.append((s, r0))
    a = jnp.dot(jnp.concatenate(terms, axis=0).astype(BF16), ones, preferred_element_type=F32)
    halves = {}
    for n, (s, r0) in enumerate(plan):
        term = a[n * SUBLANES:(n + 1) * SUBLANES] * vi[s:s + 1]
        halves[r0] = halves[r0] + term if r0 in halves else term
    return halves


def _hgrn_kernel(qf_ref, ff_ref, vf_ref, qb_ref, fb_ref, vb_ref, lb_ref, *rest):
    of_ref, ob_ref, st_ref = rest[-3:]

    @pl.when(pl.program_id(1) == 0)
    def _():
        st_ref[...] = jnp.zeros_like(st_ref)

    ones = jnp.ones((A_DIM, LANES), BF16)
    dirs = ((qf_ref, ff_ref, vf_ref, of_ref, False), (qb_ref, fb_ref, vb_ref, ob_ref, True))
    work = []
    for di, (q_ref, f_ref, v_ref, o_ref, rev) in enumerate(dirs):
        for h in range(A_HEADS):
            sl = slice(h * A_DIM, (h + 1) * A_DIM)
            kk, b = _gla_gates(f_ref[:, sl], lb_ref[di:di + 1, sl], rev)
            work.append(dict(di=di, h=h, sl=sl, rev=rev, o_ref=o_ref, q=q_ref[:, sl], v=v_ref[:, sl],
                             kk=kk, b=b))
    for w in work:
        w["o"], st_new = _gla_state(w["q"], w["v"], w["kk"], w["b"], st_ref[w["di"], w["h"]], w["rev"])
        st_ref[w["di"], w["h"]] = st_new
    for i in range(GLA_CHUNK // GLA_SUB):
        lo = i * GLA_SUB
        early = [_gla_earlier_blocks(i, w["q"], w["v"], w["kk"], w["b"], w["rev"]) for w in work]
        diag = [_gla_diagonal(i, w["q"], w["v"], w["kk"], w["b"], ones, w["rev"]) for w in work]
        for w, e, dg in zip(work, early, diag):
            oi = w["o"][lo:lo + GLA_SUB]
            if e is not None:
                oi = oi + e
            for r0, val in dg.items():
                w["o_ref"][lo + r0:lo + r0 + SUBLANES, w["sl"]] = oi[r0:r0 + SUBLANES] + val


def _hgrn_scan(ha, lb, seqs, t, first_seq, prev):
    n = ha.shape[0]
    nc = t // GLA_CHUNK
    blk = (None, GLA_CHUNK, A_WIDTH)
    fwd = lambda col: pl.BlockSpec(blk, lambda i, c: (i + first_seq, c, col))
    bwd = lambda col: pl.BlockSpec(blk, lambda i, c: (i + first_seq, nc - 1 - c, col))
    out = jax.ShapeDtypeStruct((n // t, t, A_WIDTH), F32)
    hv = _seq_view(ha, t)
    extra, extra_specs, aliases = _carry_output(
        None if prev is None else [_seq_view(p, t) for p in prev], 7)
    o_f, o_b = pl.pallas_call(
        _hgrn_kernel,
        grid=(seqs, nc),
        in_specs=[fwd(0), fwd(1), fwd(3), bwd(0), bwd(2), bwd(3),
                  pl.BlockSpec((2, A_WIDTH), lambda i, c: (0, 0))] + extra_specs,
        out_specs=[fwd(0), bwd(0)],
        out_shape=[out, out],
        scratch_shapes=[pltpu.VMEM((2, A_HEADS, A_DIM, A_DIM), F32)],
        input_output_aliases=aliases,
        compiler_params=_params("parallel", "arbitrary"),
    )(hv, hv, hv, hv, hv, hv, lb, *extra)
    return o_f.reshape(n, A_WIDTH), o_b.reshape(n, A_WIDTH)


def _swiglu_body(x, wg, wu, wd):
    g = jnp.dot(x, wg, preferred_element_type=F32)
    u = jnp.dot(x, wu, preferred_element_type=F32)
    hid = (g * jax.nn.sigmoid(g) * u).astype(BF16)
    return jnp.dot(hid, wd, preferred_element_type=F32)


def _combine_kernel(y_ref, w_ref, x_ref, xb_ref, wg_ref, wu_ref, wd_ref, g_ref, b_ref, o_ref, ob_ref):
    w = w_ref[...]
    f = _swiglu_body(xb_ref[...], wg_ref[...], wu_ref[...], wd_ref[...])
    for k in range(TOP_K):
        f = f + y_ref[k].astype(F32) * w[:, k:k + 1]
    out = _layer_norm(ALPHA * x_ref[...] + f, g_ref[...], b_ref[...])
    o_ref[...] = out
    ob_ref[...] = out.astype(BF16)


def _moe_combine(y_tok, w_tok, x, xb, wg, wu, wd, g, b, tm=256):
    n, d = x.shape
    f = wg.shape[1]
    tm = _tile(n, tm)
    return pl.pallas_call(
        _combine_kernel,
        grid=(n // tm,),
        in_specs=[pl.BlockSpec((TOP_K, tm, d), lambda i: (0, i, 0)), _rows(tm, TOP_K), _rows(tm, d),
                  _rows(tm, d), _const((d, f)), _const((d, f)), _const((f, d)), _const((1, d)),
                  _const((1, d))],
        out_specs=[_rows(tm, d), _rows(tm, d)],
        out_shape=[jax.ShapeDtypeStruct((n, d), F32), jax.ShapeDtypeStruct((n, d), BF16)],
        compiler_params=_params("parallel"),
    )(y_tok, w_tok, x, xb, wg, wu, wd, g.reshape(1, d), b.reshape(1, d))


def _grouped_kernel(blk_e_ref, nused_ref, x_ref, wg_ref, wu_ref, wd_ref, o_ref):
    @pl.when(pl.program_id(0) < nused_ref[0])
    def _():
        o_ref[...] = _swiglu_body(x_ref[...], wg_ref[0], wu_ref[0], wd_ref[0]).astype(o_ref.dtype)


def _grouped_experts(x_sorted, blk_e, nused, wg, wu, wd):
    p, d = x_sorted.shape
    f = wg.shape[2]
    nblk = p // MOE_ROWS
    row = lambda i, be, nu: (jnp.minimum(i, nu[0] - 1), 0)
    exp = lambda i, be, nu: (be[jnp.minimum(i, nu[0] - 1)], 0, 0)
    return pl.pallas_call(
        _grouped_kernel,
        grid_spec=pltpu.PrefetchScalarGridSpec(
            num_scalar_prefetch=2,
            grid=(nblk,),
            in_specs=[pl.BlockSpec((MOE_ROWS, d), row),
                      pl.BlockSpec((1, d, f), exp),
                      pl.BlockSpec((1, d, f), exp),
                      pl.BlockSpec((1, f, d), exp)],
            out_specs=pl.BlockSpec((MOE_ROWS, d), row)),
        out_shape=jax.ShapeDtypeStruct((p, d), BF16),
        compiler_params=_params("arbitrary"),
    )(blk_e, nused, x_sorted, wg, wu, wd)


def _moe_block(x, xb, w_router, router_bias, w_gate, w_up, w_down, ws_gate, ws_up, ws_down, ln_g, ln_b):
    n, d = x.shape
    tm = TOKEN_TILE
    top_e, top_w, tile_cnt = _route(x, w_router, router_bias, tm)
    tile_cnt = tile_cnt.astype(jnp.int32)
    counts = jnp.sum(tile_cnt, axis=0)
    padded = (counts + MOE_ROWS - 1) // MOE_ROWS * MOE_ROWS
    pend = jnp.cumsum(padded)
    pstart = pend - padded
    tile_base = pstart[None, :] + jnp.cumsum(tile_cnt, axis=0) - tile_cnt
    dest = _dest_rows(top_e, tile_base, tm)
    nblk = (n * TOP_K) // MOE_ROWS + N_EXPERTS
    blk_e = jnp.minimum(jnp.searchsorted(pend, jnp.arange(nblk) * MOE_ROWS, side='right'),
                        N_EXPERTS - 1).astype(jnp.int32)
    nused = (pend[-1] // MOE_ROWS).astype(jnp.int32).reshape(1)
    tok = jnp.broadcast_to(jnp.arange(n, dtype=jnp.int32)[None, :], (TOP_K, n))
    pad_tok = jnp.arange(nblk * MOE_ROWS, dtype=jnp.int32) % n
    slot_tok = pad_tok.at[dest.reshape(-1)].set(tok.reshape(-1), unique_indices=True,
                                                mode="promise_in_bounds")
    x_sorted = xb.at[slot_tok].get(mode="promise_in_bounds")
    y_sorted = _grouped_experts(x_sorted, blk_e, nused, w_gate.astype(BF16), w_up.astype(BF16),
                                w_down.astype(BF16))
    y_tok = y_sorted.at[dest.reshape(-1)].get(mode="promise_in_bounds").reshape(TOP_K, n, d)
    return _moe_combine(y_tok, top_w.T, x, xb, ws_gate.astype(BF16), ws_up.astype(BF16),
                        ws_down.astype(BF16), ln_g, ln_b)


def _per_group(shapes, fn):
    out = None
    tok = 0
    for (b, t) in shapes:
        assert tok % t == 0
        out = fn(b, t, tok // t, out)
        tok += b * t
    return out


def _even_mixer(x, xb, shapes, tables, w_in, lb, hgrn_norm_w, q_norm_w, k_norm_w, w_out, ln_g, ln_b):
    ha, q, k, v = _even_prep(xb, shapes, w_in, q_norm_w, k_norm_w, tables)
    lb2 = lb.reshape(2, A_WIDTH)
    o_f, o_b = _per_group(shapes, lambda b, t, s0, prev: _hgrn_scan(ha, lb2, b, t, s0, prev))
    att = _per_group(shapes, lambda b, t, s0, prev: _attention(
        q, k, v, b, t, s0, prev, kv_heads=B_KV_HEADS, groups=B_GROUP, dq=B_HEAD_DIM, dv=B_HEAD_DIM))
    return _even_out(o_f, o_b, ha, att, hgrn_norm_w, w_out, x, ln_g, ln_b)


def _mla_mixer(x, xb, shapes, tables, w_in, q_a_norm_w, w_q_b, kv_a_norm_w, w_kv_b, w_out, ln_g, ln_b):
    q, k, v = _mla_prep(xb, shapes, w_in, q_a_norm_w, w_q_b, kv_a_norm_w, w_kv_b, tables)
    att = _per_group(shapes, lambda b, t, s0, prev: _attention(
        q, k, v, b, t, s0, prev, kv_heads=C_HEADS, groups=1, dq=C_QK_PAD, dv=C_V))
    return _matmul_residual_ln(att, w_out.astype(BF16), x, ln_g, ln_b)


def _trunks(xs, params):
    (w_in_even, lb_logits, hgrn_norm_w, q_norm_w, k_norm_w, w_out_even,
     w_in_odd, q_a_norm_w, w_q_b, kv_a_norm_w, w_kv_b, w_out_odd,
     w_router, router_bias, w_gate, w_up, w_down, ws_gate, ws_up, ws_down,
     ln_mix_g, ln_mix_b, ln_ffn_g, ln_ffn_b) = params
    shapes = [(x.shape[0], x.shape[1]) for x in xs]
    for (b, t) in shapes:
        assert t % TOKEN_TILE == 0 and t % GRID_W == 0
    d = xs[0].shape[-1]
    x = jnp.concatenate([a.reshape(-1, d) for a in xs], axis=0)
    xb = x.astype(BF16)
    t_max = max(t for _, t in shapes)
    tables_b = _rope_cs(t_max, B_HEAD_DIM)
    tables_c = _rope_cs(t_max, C_ROPE)
    lb_all = jnp.cumsum(jax.nn.softmax(lb_logits.astype(F32), axis=0), axis=0)
    for l in range(DEPTH):
        j = l // 2
        if l % 2 == 0:
            x, xb = _even_mixer(x, xb, shapes, tables_b, w_in_even[j], lb_all[j], hgrn_norm_w[j],
                                q_norm_w[j], k_norm_w[j], w_out_even[j], ln_mix_g[l], ln_mix_b[l])
        else:
            x, xb = _mla_mixer(x, xb, shapes, tables_c, w_in_odd[j], q_a_norm_w[j], w_q_b[j],
                               kv_a_norm_w[j], w_kv_b[j], w_out_odd[j], ln_mix_g[l], ln_mix_b[l])
        x, xb = _moe_block(x, xb, w_router[l], router_bias[l], w_gate[l], w_up[l], w_down[l],
                           ws_gate[l], ws_up[l], ws_down[l], ln_ffn_g[l], ln_ffn_b[l])
    outs = []
    off = 0
    for (b, t) in shapes:
        outs.append(x[off:off + b * t].reshape(b, t, d))
        off += b * t
    return tuple(outs)


def kernel(x_prompt, x_sample, w_in_even, lb_logits, hgrn_norm_w, q_norm_w, k_norm_w, w_out_even,
           w_in_odd, q_a_norm_w, w_q_b, kv_a_norm_w, w_kv_b, w_out_odd,
           w_router, router_bias, w_gate, w_up, w_down, ws_gate, ws_up, ws_down,
           ln_mix_g, ln_mix_b, ln_ffn_g, ln_ffn_b):
    params = (w_in_even, lb_logits, hgrn_norm_w, q_norm_w, k_norm_w, w_out_even,
              w_in_odd, q_a_norm_w, w_q_b, kv_a_norm_w, w_kv_b, w_out_odd,
              w_router, router_bias, w_gate, w_up, w_down, ws_gate, ws_up, ws_down,
              ln_mix_g, ln_mix_b, ln_ffn_g, ln_ffn_b)
    return _trunks([x_prompt, x_sample], params)
```

```python
import functools

import jax
import jax.numpy as jnp
from jax import lax
from jax.experimental import pallas as pl
from jax.experimental.pallas import tpu as pltpu

F32 = jnp.float32
BF16 = jnp.bfloat16

GRID_W = 64
ROPE_THETA = 10000.0
RMS_EPS = 1e-6
LN_EPS = 1e-5
DEPTH = 2
ALPHA = (2 * DEPTH) ** 0.25

A_HEADS = 4
A_DIM = 128
A_WIDTH = A_HEADS * A_DIM
A_COLS = 5 * A_WIDTH
B_HEADS = 4
B_KV_HEADS = 2
B_GROUP = B_HEADS // B_KV_HEADS
B_HEAD_DIM = 128
B_WIDTH = B_HEADS * B_HEAD_DIM
B_KVW = B_KV_HEADS * B_HEAD_DIM
C_HEADS = 8
C_NOPE = 128
C_ROPE = 64
C_V = 128
C_Q_LORA = 384
C_KV_LORA = 256
C_QK_PAD = 256
N_EXPERTS = 256
TOP_K = 8
N_GROUPS = 8
TOPK_GROUPS = 4
ROUTED_SCALE = 2.5

V7X_VMEM_LIMIT_BYTES = 56 * 1024 * 1024
LANES = 128
SUBLANES = 8

TOKEN_TILE = 512
GLA_CHUNK = 64
GLA_SUB = 16
MOE_ROWS = 512
MOE_PARTS = 4
NEG_BIG = -1e30
LOG2_E = 1.4426950408889634

_NT = (((1,), (1,)), ((), ()))


def _params(*semantics):
    return pltpu.CompilerParams(dimension_semantics=semantics,
                                vmem_limit_bytes=V7X_VMEM_LIMIT_BYTES)


def _tile(n, want):
    t = min(n, want)
    assert n % t == 0, (n, t)
    return t


def _rows(tm, width):
    return pl.BlockSpec((tm, width), lambda i: (i, 0))


def _const(shape):
    return pl.BlockSpec(shape, lambda i: (0,) * len(shape))


def _layer_norm(x, g, b):
    mu = jnp.mean(x, axis=-1, keepdims=True)
    xc = x - mu
    var = jnp.mean(xc * xc, axis=-1, keepdims=True)
    return xc * lax.rsqrt(var + LN_EPS) * g + b


def _inv_rms(x):
    return lax.rsqrt(jnp.mean(x * x, axis=-1, keepdims=True) + RMS_EPS)


def _rope_swap(n):
    q = n // 4
    a = jnp.arange(q)
    return jnp.concatenate([a + q, a, a + 3 * q, a + 2 * q])


def _rope_cs(t, rot_dim):
    row = (jnp.arange(t) // GRID_W).astype(F32)
    col = (jnp.arange(t) % GRID_W).astype(F32)
    half = rot_dim // 2
    inv = ROPE_THETA ** (-jnp.arange(0, half, 2, dtype=F32) / half)
    ar = row[:, None] * inv[None, :]
    ac = col[:, None] * inv[None, :]
    c = jnp.concatenate([jnp.cos(ar), jnp.cos(ar), jnp.cos(ac), jnp.cos(ac)], axis=1)
    s = jnp.concatenate([-jnp.sin(ar), jnp.sin(ar), -jnp.sin(ac), jnp.sin(ac)], axis=1)
    pad = ((0, 0), (0, LANES - rot_dim))
    return jnp.pad(c, pad), jnp.pad(s, pad)


def _pos_block(shapes, tm):
    groups = []
    first = 0
    for (b, t) in shapes:
        groups.append((first, t // tm))
        first += b * t // tm

    def index_map(i):
        first_tile, per_seq = groups[0]
        blk = (i - first_tile) % per_seq
        for first_tile, per_seq in groups[1:]:
            blk = jnp.where(i >= first_tile, (i - first_tile) % per_seq, blk)
        return (blk, 0)

    return index_map


def _even_prep_kernel(x_ref, w_ref, qw_ref, kw_ref, c_ref, s_ref, ha_ref, q_ref, k_ref, v_ref):
    xb = x_ref[...]
    for j in range(0, A_COLS, A_WIDTH):
        ha_ref[:, j:j + A_WIDTH] = jnp.dot(xb, w_ref[:, j:j + A_WIDTH], preferred_element_type=F32)
    c = c_ref[...]
    s = s_ref[...]
    q0, k0, v0 = A_COLS, A_COLS + B_WIDTH, A_COLS + B_WIDTH + B_KVW
    qs0 = v0 + B_KVW
    ks0 = qs0 + B_WIDTH
    d = B_HEAD_DIM

    def normed_rope(col, col_sw, heads, w2_ref, scale, o_ref):
        y = jnp.dot(xb, w_ref[:, col:col + heads * d], preferred_element_type=F32)
        ysw = jnp.dot(xb, w_ref[:, col_sw:col_sw + heads * d], preferred_element_type=F32)
        for h in range(heads):
            sl = slice(h * d, (h + 1) * d)
            yh = y[:, sl]
            rot = yh * (w2_ref[0:1] * c) + ysw[:, sl] * (w2_ref[1:2] * s)
            o_ref[:, sl] = (rot * (_inv_rms(yh) * scale)).astype(BF16)

    normed_rope(q0, qs0, B_HEADS, qw_ref, B_HEAD_DIM ** -0.5 * LOG2_E, q_ref)
    normed_rope(k0, ks0, B_KV_HEADS, kw_ref, 1.0, k_ref)
    v_ref[...] = jnp.dot(xb, w_ref[:, v0:v0 + B_KVW], preferred_element_type=F32).astype(BF16)


def _even_prep(xb, shapes, w_in, q_norm_w, k_norm_w, tables):
    n, dm = xb.shape
    tm = TOKEN_TILE
    swap = _rope_swap(B_HEAD_DIM)
    q0, k0 = A_COLS, A_COLS + B_WIDTH
    wq = w_in[:, q0:q0 + B_WIDTH].reshape(dm, B_HEADS, B_HEAD_DIM)[:, :, swap].reshape(dm, B_WIDTH)
    wk = w_in[:, k0:k0 + B_KVW].reshape(dm, B_KV_HEADS, B_HEAD_DIM)[:, :, swap].reshape(dm, B_KVW)
    w_ext = jnp.concatenate([w_in, wq, wk], axis=1).astype(BF16)
    qw = jnp.stack([q_norm_w, q_norm_w[swap]]).astype(F32)
    kw = jnp.stack([k_norm_w, k_norm_w[swap]]).astype(F32)
    pos = pl.BlockSpec((tm, LANES), _pos_block(shapes, tm))
    return pl.pallas_call(
        _even_prep_kernel,
        grid=(n // tm,),
        in_specs=[_rows(tm, dm), _const(w_ext.shape), _const((2, B_HEAD_DIM)), _const((2, B_HEAD_DIM)),
                  pos, pos],
        out_specs=[_rows(tm, A_COLS), _rows(tm, B_WIDTH), _rows(tm, B_KVW), _rows(tm, B_KVW)],
        out_shape=[jax.ShapeDtypeStruct((n, A_COLS), F32), jax.ShapeDtypeStruct((n, B_WIDTH), BF16),
                   jax.ShapeDtypeStruct((n, B_KVW), BF16), jax.ShapeDtypeStruct((n, B_KVW), BF16)],
        compiler_params=_params("parallel"),
    )(xb, w_ext, qw, kw, tables[0], tables[1])


def _even_out_kernel(of_ref, ob_ref, g_ref, att_ref, nw_ref, w_ref, r_ref, lg_ref, lb_ref, o_ref, ob16_ref):
    acc = ALPHA * r_ref[...] + jnp.dot(att_ref[...], w_ref[A_WIDTH:, :], preferred_element_type=F32)
    nw = nw_ref[...]
    parts = []
    for h in range(A_HEADS):
        sl = slice(h * A_DIM, (h + 1) * A_DIM)
        o = of_ref[:, sl] + ob_ref[:, sl]
        g = g_ref[:, sl]
        parts.append((o * _inv_rms(o) * nw * (g * jax.nn.sigmoid(g))).astype(BF16))
    acc += jnp.dot(jnp.concatenate(parts, axis=1), w_ref[:A_WIDTH, :], preferred_element_type=F32)
    out = _layer_norm(acc, lg_ref[...], lb_ref[...])
    o_ref[...] = out
    ob16_ref[...] = out.astype(BF16)


def _even_out(o_f, o_b, ha, att, hgrn_norm_w, w_out, x, ln_g, ln_b):
    n, d = x.shape
    tm = TOKEN_TILE
    gate = pl.BlockSpec((tm, A_WIDTH), lambda i: (i, 4))
    return pl.pallas_call(
        _even_out_kernel,
        grid=(n // tm,),
        in_specs=[_rows(tm, A_WIDTH), _rows(tm, A_WIDTH), gate, _rows(tm, B_WIDTH), _const((1, A_DIM)),
                  _const(w_out.shape), _rows(tm, d), _const((1, d)), _const((1, d))],
        out_specs=[_rows(tm, d), _rows(tm, d)],
        out_shape=[jax.ShapeDtypeStruct((n, d), F32), jax.ShapeDtypeStruct((n, d), BF16)],
        compiler_params=_params("parallel"),
    )(o_f, o_b, ha, att, hgrn_norm_w.reshape(1, A_DIM).astype(F32), w_out.astype(BF16), x,
      ln_g.reshape(1, d), ln_b.reshape(1, d))


def _mla_prep_kernel(x_ref, win_ref, qn_ref, kn_ref, wq_ref, wkv_ref, c_ref, s_ref, q_ref, k_ref, v_ref):
    xb = x_ref[...]
    c = c_ref[...]
    s = s_ref[...]
    hc = jnp.dot(xb, win_ref[...], preferred_element_type=F32)
    cq = hc[:, :C_Q_LORA]
    cq = (cq * _inv_rms(cq) * qn_ref[...]).astype(BF16)
    ckv = hc[:, C_Q_LORA:C_Q_LORA + C_KV_LORA]
    ckv = (ckv * _inv_rms(ckv) * kn_ref[...]).astype(BF16)
    kr0 = C_Q_LORA + C_KV_LORA
    k_rope = (hc[:, kr0:kr0 + LANES] * c + hc[:, kr0 + LANES:kr0 + 2 * LANES] * s).astype(BF16)
    scale = (C_NOPE + C_ROPE) ** -0.5 * LOG2_E
    hw = C_HEADS * C_NOPE
    q_nope = jnp.dot(cq, wq_ref[:, :hw], preferred_element_type=F32)
    q_r = jnp.dot(cq, wq_ref[:, hw:2 * hw], preferred_element_type=F32)
    q_rs = jnp.dot(cq, wq_ref[:, 2 * hw:], preferred_element_type=F32)
    k_nope = jnp.dot(ckv, wkv_ref[:, :hw], preferred_element_type=F32)
    v_ref[...] = jnp.dot(ckv, wkv_ref[:, hw:], preferred_element_type=F32).astype(BF16)
    for h in range(C_HEADS):
        sl = slice(h * LANES, (h + 1) * LANES)
        lo = h * C_QK_PAD
        q_ref[:, lo:lo + LANES] = (q_nope[:, sl] * scale).astype(BF16)
        q_ref[:, lo + LANES:lo + 2 * LANES] = ((q_r[:, sl] * c + q_rs[:, sl] * s) * scale).astype(BF16)
        k_ref[:, lo:lo + LANES] = k_nope[:, sl].astype(BF16)
        k_ref[:, lo + LANES:lo + 2 * LANES] = k_rope


def _mla_prep(xb, shapes, w_in, q_a_norm_w, w_q_b, kv_a_norm_w, w_kv_b, tables):
    n, dm = xb.shape
    tm = TOKEN_TILE
    swap = _rope_swap(C_ROPE)
    lane_pad = lambda w: jnp.pad(w, ((0, 0),) * (w.ndim - 1) + ((0, LANES - C_ROPE),))
    kr0 = C_Q_LORA + C_KV_LORA
    w_kr = w_in[:, kr0:]
    win = jnp.concatenate([w_in[:, :kr0], lane_pad(w_kr), lane_pad(w_kr[:, swap])], axis=1).astype(BF16)
    wq = w_q_b.reshape(C_Q_LORA, C_HEADS, C_NOPE + C_ROPE)
    wq_r = wq[:, :, C_NOPE:]
    hw = C_HEADS * LANES
    wq_all = jnp.concatenate([wq[:, :, :C_NOPE].reshape(C_Q_LORA, hw),
                              lane_pad(wq_r).reshape(C_Q_LORA, hw),
                              lane_pad(wq_r[:, :, swap]).reshape(C_Q_LORA, hw)], axis=1).astype(BF16)
    wkv = w_kv_b.reshape(C_KV_LORA, C_HEADS, C_NOPE + C_V)
    wkv_all = jnp.concatenate([wkv[:, :, :C_NOPE].reshape(C_KV_LORA, hw),
                               wkv[:, :, C_NOPE:].reshape(C_KV_LORA, hw)], axis=1).astype(BF16)
    pos = pl.BlockSpec((tm, LANES), _pos_block(shapes, tm))
    qk = jax.ShapeDtypeStruct((n, C_HEADS * C_QK_PAD), BF16)
    return pl.pallas_call(
        _mla_prep_kernel,
        grid=(n // tm,),
        in_specs=[_rows(tm, dm), _const(win.shape), _const((1, C_Q_LORA)), _const((1, C_KV_LORA)),
                  _const(wq_all.shape), _const(wkv_all.shape), pos, pos],
        out_specs=[_rows(tm, C_HEADS * C_QK_PAD), _rows(tm, C_HEADS * C_QK_PAD), _rows(tm, C_HEADS * C_V)],
        out_shape=[qk, qk, jax.ShapeDtypeStruct((n, C_HEADS * C_V), BF16)],
        compiler_params=_params("parallel"),
    )(xb, win, q_a_norm_w.reshape(1, -1).astype(F32), kv_a_norm_w.reshape(1, -1).astype(F32),
      wq_all, wkv_all, tables[0], tables[1])


def _mm_ln_kernel(x_ref, w_ref, r_ref, g_ref, b_ref, o_ref, ob_ref):
    y = ALPHA * r_ref[...] + jnp.dot(x_ref[...], w_ref[...], preferred_element_type=F32)
    out = _layer_norm(y, g_ref[...], b_ref[...])
    o_ref[...] = out
    ob_ref[...] = out.astype(BF16)


def _matmul_residual_ln(x, w, resid, g, b):
    m, k = x.shape
    d = w.shape[1]
    tm = TOKEN_TILE
    return pl.pallas_call(
        _mm_ln_kernel,
        grid=(m // tm,),
        in_specs=[_rows(tm, k), _const((k, d)), _rows(tm, d), _const((1, d)), _const((1, d))],
        out_specs=[_rows(tm, d), _rows(tm, d)],
        out_shape=[jax.ShapeDtypeStruct((m, d), F32), jax.ShapeDtypeStruct((m, d), BF16)],
        compiler_params=_params("parallel"),
    )(x, w, resid, g.reshape(1, d), b.reshape(1, d))


def _row_iota(rows, cols):
    return lax.broadcasted_iota(jnp.int32, (rows, cols), 0).astype(F32)


def _first_max(cur, iota, n):
    m = jnp.max(cur, axis=0, keepdims=True)
    idx = jnp.min(jnp.where(cur == m, iota, float(n)), axis=0, keepdims=True)
    return m, idx


def _route_kernel(x_ref, wh_ref, wl_ref, bias_ref, e_ref, w_ref, cnt_ref):
    x = x_ref[...]
    tm = x.shape[0]
    xh = x.astype(BF16)
    xl = (x - xh.astype(F32)).astype(BF16)
    wh = wh_ref[...]
    logits = lax.dot_general(wh, xh, _NT, preferred_element_type=F32)
    logits += lax.dot_general(wh, xl, _NT, preferred_element_type=F32)
    logits += lax.dot_general(wl_ref[...], xh, _NT, preferred_element_type=F32)
    scores = jax.nn.sigmoid(logits)
    choice = scores + bias_ref[...]
    gsz = N_EXPERTS // N_GROUPS
    sub_iota = _row_iota(gsz, tm)
    group_rows = []
    for g in range(N_GROUPS):
        grp = choice[g * gsz:(g + 1) * gsz]
        m1, first = _first_max(grp, sub_iota, gsz)
        m2 = jnp.max(jnp.where(sub_iota == first, -jnp.inf, grp), axis=0, keepdims=True)
        group_rows.append(m1 + m2)
    cur = jnp.concatenate(group_rows, axis=0)
    g_iota = _row_iota(N_GROUPS, tm)
    sel = jnp.zeros((N_GROUPS, tm), F32)
    for _ in range(TOPK_GROUPS):
        _, gi = _first_max(cur, g_iota, N_GROUPS)
        pick = g_iota == gi
        sel = jnp.where(pick, 1.0, sel)
        cur = jnp.where(pick, -jnp.inf, cur)
    masked = jnp.concatenate(
        [jnp.where(sel[g:g + 1] > 0.5, choice[g * gsz:(g + 1) * gsz], -jnp.inf)
         for g in range(N_GROUPS)], axis=0)
    e_iota = _row_iota(N_EXPERTS, tm)
    cur = masked
    ids, vals = [], []
    for _ in range(TOP_K):
        _, ei = _first_max(cur, e_iota, N_EXPERTS)
        pick = e_iota == ei
        vals.append(jnp.sum(jnp.where(pick, scores, 0.0), axis=0, keepdims=True))
        ids.append(ei)
        cur = jnp.where(pick, -jnp.inf, cur)
    member = jnp.where((masked > -jnp.inf) & (cur == -jnp.inf), 1.0, 0.0).astype(BF16)
    s = jnp.concatenate(vals, axis=0)
    e_ref[...] = jnp.concatenate(ids, axis=0).astype(jnp.int32)
    w_ref[...] = s / (jnp.sum(s, axis=0, keepdims=True) + 1e-20) * ROUTED_SCALE
    cnt_ref[...] = lax.dot_general(jnp.ones((8, tm), BF16), member, _NT, preferred_element_type=F32)


def _route(x, w_router, router_bias, tm):
    n, d = x.shape
    wt = w_router.T
    wh = wt.astype(BF16)
    wl = (wt - wh.astype(F32)).astype(BF16)
    e, w, cnt = pl.pallas_call(
        _route_kernel,
        grid=(n // tm,),
        in_specs=[_rows(tm, d), _const((N_EXPERTS, d)), _const((N_EXPERTS, d)), _const((N_EXPERTS, 1))],
        out_specs=[pl.BlockSpec((TOP_K, tm), lambda i: (0, i)),
                   pl.BlockSpec((TOP_K, tm), lambda i: (0, i)),
                   pl.BlockSpec((None, 8, N_EXPERTS), lambda i: (i, 0, 0))],
        out_shape=[jax.ShapeDtypeStruct((TOP_K, n), jnp.int32),
                   jax.ShapeDtypeStruct((TOP_K, n), F32),
                   jax.ShapeDtypeStruct((n // tm, 8, N_EXPERTS), F32)],
        compiler_params=_params("parallel"),
    )(x, wh, wl, router_bias.reshape(N_EXPERTS, 1).astype(F32))
    return e, w, cnt[:, 0, :]


def _dest_kernel(e_ref, base_ref, d_ref):
    e = e_ref[...].astype(F32)
    tm = e.shape[1]
    e_iota = _row_iota(N_EXPERTS, tm)
    picks = [e_iota == e[k:k + 1] for k in range(TOP_K)]
    member = jnp.zeros((N_EXPERTS, tm), F32)
    for p in picks:
        member = jnp.where(p, 1.0, member)
    r = lax.broadcasted_iota(jnp.int32, (tm, tm), 0)
    c = lax.broadcasted_iota(jnp.int32, (tm, tm), 1)
    upper = jnp.where(r <= c, 1.0, 0.0).astype(BF16)
    prefix = jnp.dot(member.astype(BF16), upper, preferred_element_type=F32)
    val = prefix + (base_ref[...] - 1.0)
    rows = [jnp.sum(jnp.where(p, val, 0.0), axis=0, keepdims=True) for p in picks]
    d_ref[...] = jnp.concatenate(rows, axis=0).astype(jnp.int32)


def _dest_rows(top_e, tile_base, tm):
    k, n = top_e.shape
    return pl.pallas_call(
        _dest_kernel,
        grid=(n // tm,),
        in_specs=[pl.BlockSpec((k, tm), lambda i: (0, i)),
                  pl.BlockSpec((None, N_EXPERTS, 1), lambda i: (i, 0, 0))],
        out_specs=pl.BlockSpec((k, tm), lambda i: (0, i)),
        out_shape=jax.ShapeDtypeStruct((k, n), jnp.int32),
        compiler_params=_params("parallel"),
    )(top_e, tile_base.astype(F32)[:, :, None])


def _flash_kernel(q_ref, k_ref, v_ref, *rest, groups, dq, dv, tk):
    o_ref = rest[-1]
    tq = q_ref.shape[0]
    t = k_ref.shape[0]
    qs = [q_ref[:, g * dq:(g + 1) * dq] for g in range(groups)]

    def body(c, carry):
        start = pl.multiple_of(c * tk, tk)
        k = k_ref[pl.ds(start, tk), :]
        v = v_ref[pl.ds(start, tk), :]
        out = []
        for g in range(groups):
            m, l, acc = carry[g]
            s = lax.dot_general(qs[g], k, _NT, preferred_element_type=F32)
            m_new = jnp.maximum(m, jnp.max(s, axis=-1, keepdims=True))
            alpha = jnp.exp2(m - m_new)
            p = jnp.exp2(s - m_new)
            l = alpha * l + jnp.sum(p, axis=-1, keepdims=True)
            acc = alpha * acc + jnp.dot(p.astype(BF16), v, preferred_element_type=F32)
            out.append((m_new, l, acc))
        return tuple(out)

    init = tuple((jnp.full((tq, 1), NEG_BIG, F32), jnp.zeros((tq, 1), F32),
                  jnp.zeros((tq, dv), F32)) for _ in range(groups))
    res = lax.fori_loop(0, t // tk, body, init, unroll=2)
    for g in range(groups):
        _, l, acc = res[g]
        o_ref[:, g * dv:(g + 1) * dv] = (acc / l).astype(o_ref.dtype)


def _seq_view(a, t):
    return a.reshape(a.shape[0] // t, t, a.shape[1])


def _carry_output(prev, n_in):
    if prev is None:
        return [], [], {}
    prev = list(prev)
    return prev, [pl.BlockSpec(memory_space=pl.ANY)] * len(prev), {n_in + k: k for k in range(len(prev))}


def _attention(q, k, v, seqs, t, first_seq, prev, *, kv_heads, groups, dq, dv, tq=512, tk=1024):
    n = q.shape[0]
    tq = _tile(t, tq)
    tk = _tile(t, tk)
    kern = functools.partial(_flash_kernel, groups=groups, dq=dq, dv=dv, tk=tk)
    extra, extra_specs, aliases = _carry_output(None if prev is None else [_seq_view(prev, t)], 3)
    out = pl.pallas_call(
        kern,
        grid=(seqs, kv_heads, t // tq),
        in_specs=[pl.BlockSpec((None, tq, groups * dq), lambda i, h, j: (i + first_seq, j, h)),
                  pl.BlockSpec((None, t, dq), lambda i, h, j: (i + first_seq, 0, h)),
                  pl.BlockSpec((None, t, dv), lambda i, h, j: (i + first_seq, 0, h))] + extra_specs,
        out_specs=pl.BlockSpec((None, tq, groups * dv), lambda i, h, j: (i + first_seq, j, h)),
        out_shape=jax.ShapeDtypeStruct((n // t, t, kv_heads * groups * dv), BF16),
        input_output_aliases=aliases,
        compiler_params=_params("parallel", "parallel", "arbitrary"),
    )(_seq_view(q, t), _seq_view(k, t), _seq_view(v, t), *extra)
    return out.reshape(n, -1)


def _cumsum_time(g, rev):
    c = g.shape[0]
    row = lax.broadcasted_iota(jnp.int32, (c, c), 0)
    col = lax.broadcasted_iota(jnp.int32, (c, c), 1)
    tri = jnp.where((col >= row) if rev else (col <= row), 1.0, 0.0).astype(BF16)
    g1 = g.astype(BF16)
    r1 = g - g1.astype(F32)
    g2 = r1.astype(BF16)
    g3 = (r1 - g2.astype(F32)).astype(BF16)
    out = jnp.dot(tri, g1, preferred_element_type=F32)
    out += jnp.dot(tri, g2, preferred_element_type=F32)
    out += jnp.dot(tri, g3, preferred_element_type=F32)
    return out


def _gla_gates(fl, lb, rev):
    fg = lb + (1.0 - lb) * jax.nn.sigmoid(fl)
    return 1.0 - fg, _cumsum_time(jnp.log2(fg), rev)


def _gla_state(q, v, kk, b, st, rev):
    b_edge = b[0:1] if rev else b[GLA_CHUNK - 1:GLA_CHUNK]
    o = lax.dot_general((q * jnp.exp2(b)).astype(BF16), st.astype(BF16), _NT,
                        preferred_element_type=F32)
    kdec = (kk * jnp.exp2(b_edge - b)).astype(BF16)
    st_new = st * jnp.exp2(b_edge) + lax.dot_general(
        v.astype(BF16), kdec, (((0,), (0,)), ((), ())), preferred_element_type=F32)
    return o, st_new


def _gla_earlier_blocks(i, q, v, kk, b, rev):
    cn, sub = GLA_CHUNK, GLA_SUB
    lo, hi = i * sub, (i + 1) * sub
    if rev:
        if hi == cn:
            return None
        ref_b, plo, phi = b[hi:hi + 1], hi, cn
    else:
        if lo == 0:
            return None
        ref_b, plo, phi = b[lo - 1:lo], 0, lo
    qd = (q[lo:hi] * jnp.exp2(b[lo:hi] - ref_b)).astype(BF16)
    kd = (kk[plo:phi] * jnp.exp2(ref_b - b[plo:phi])).astype(BF16)
    a = lax.dot_general(qd, kd, _NT, preferred_element_type=F32)
    return jnp.dot(a.astype(BF16), v[plo:phi].astype(BF16), preferred_element_type=F32)


def _gla_diagonal(i, q, v, kk, b, ones, rev):
    sub = GLA_SUB
    lo = i * sub
    bi, qi, ki, vi = b[lo:lo + sub], q[lo:lo + sub], kk[lo:lo + sub], v[lo:lo + sub]
    t_idx = lax.broadcasted_iota(jnp.int32, (SUBLANES, 1), 0)
    terms, plan = [], []
    for s in range(sub):
        for r0 in range(0, sub, SUBLANES):
            if (r0 > s) if rev else (r0 + SUBLANES - 1 < s):
                continue
            d = bi[r0:r0 + SUBLANES] - bi[s:s + 1]
            if r0 <= s < r0 + SUBLANES:
                keep = (t_idx + r0 <= s) if rev else (t_idx + r0 >= s)
                d = jnp.where(keep, d, NEG_BIG)
            terms.append(qi[r0:r0 + SUBLANES] * jnp.exp2(d) * ki[s:s + 1])
            plan.append((s, r0))
    a = jnp.dot(jnp.concatenate(terms, axis=0).astype(BF16), ones, preferred_element_type=F32)
    halves = {}
    for n, (s, r0) in enumerate(plan):
        term = a[n * SUBLANES:(n + 1) * SUBLANES] * vi[s:s + 1]
        halves[r0] = halves[r0] + term if r0 in halves else term
    return halves


def _hgrn_kernel(qf_ref, ff_ref, vf_ref, qb_ref, fb_ref, vb_ref, lb_ref, *rest):
    of_ref, ob_ref, st_ref = rest[-3:]

    @pl.when(pl.program_id(1) == 0)
    def _():
        st_ref[...] = jnp.zeros_like(st_ref)

    ones = jnp.ones((A_DIM, LANES), BF16)
    dirs = ((qf_ref, ff_ref, vf_ref, of_ref, False), (qb_ref, fb_ref, vb_ref, ob_ref, True))
    work = []
    for di, (q_ref, f_ref, v_ref, o_ref, rev) in enumerate(dirs):
        for h in range(A_HEADS):
            sl = slice(h * A_DIM, (h + 1) * A_DIM)
            kk, b = _gla_gates(f_ref[:, sl], lb_ref[di:di + 1, sl], rev)
            work.append(dict(di=di, h=h, sl=sl, rev=rev, o_ref=o_ref, q=q_ref[:, sl], v=v_ref[:, sl],
                             kk=kk, b=b))
    for w in work:
        w["o"], st_new = _gla_state(w["q"], w["v"], w["kk"], w["b"], st_ref[w["di"], w["h"]], w["rev"])
        st_ref[w["di"], w["h"]] = st_new
    for i in range(GLA_CHUNK // GLA_SUB):
        lo = i * GLA_SUB
        early = [_gla_earlier_blocks(i, w["q"], w["v"], w["kk"], w["b"], w["rev"]) for w in work]
        diag = [_gla_diagonal(i, w["q"], w["v"], w["kk"], w["b"], ones, w["rev"]) for w in work]
        for w, e, dg in zip(work, early, diag):
            oi = w["o"][lo:lo + GLA_SUB]
            if e is not None:
                oi = oi + e
            for r0, val in dg.items():
                w["o_ref"][lo + r0:lo + r0 + SUBLANES, w["sl"]] = oi[r0:r0 + SUBLANES] + val


def _hgrn_scan(ha, lb, seqs, t, first_seq, prev):
    n = ha.shape[0]
    nc = t // GLA_CHUNK
    blk = (None, GLA_CHUNK, A_WIDTH)
    fwd = lambda col: pl.BlockSpec(blk, lambda i, c: (i + first_seq, c, col))
    bwd = lambda col: pl.BlockSpec(blk, lambda i, c: (i + first_seq, nc - 1 - c, col))
    out = jax.ShapeDtypeStruct((n // t, t, A_WIDTH), F32)
    hv = _seq_view(ha, t)
    extra, extra_specs, aliases = _carry_output(
        None if prev is None else [_seq_view(p, t) for p in prev], 7)
    o_f, o_b = pl.pallas_call(
        _hgrn_kernel,
        grid=(seqs, nc),
        in_specs=[fwd(0), fwd(1), fwd(3), bwd(0), bwd(2), bwd(3),
                  pl.BlockSpec((2, A_WIDTH), lambda i, c: (0, 0))] + extra_specs,
        out_specs=[fwd(0), bwd(0)],
        out_shape=[out, out],
        scratch_shapes=[pltpu.VMEM((2, A_HEADS, A_DIM, A_DIM), F32)],
        input_output_aliases=aliases,
        compiler_params=_params("parallel", "arbitrary"),
    )(hv, hv, hv, hv, hv, hv, lb, *extra)
    return o_f.reshape(n, A_WIDTH), o_b.reshape(n, A_WIDTH)


def _swiglu_body(x, wg, wu, wd):
    g = jnp.dot(x, wg, preferred_element_type=F32)
    u = jnp.dot(x, wu, preferred_element_type=F32)
    hid = (g * jax.nn.sigmoid(g) * u).astype(BF16)
    return jnp.dot(hid, wd, preferred_element_type=F32)


def _combine_kernel(y_ref, w_ref, x_ref, xb_ref, wg_ref, wu_ref, wd_ref, g_ref, b_ref, *rest):
    o_ref, ob_ref = rest[-2:]
    w = w_ref[...]
    f = _swiglu_body(xb_ref[...], wg_ref[...], wu_ref[...], wd_ref[...])
    for k in range(TOP_K):
        f = f + y_ref[k].astype(F32) * w[:, k:k + 1]
    out = _layer_norm(ALPHA * x_ref[...] + f, g_ref[...], b_ref[...])
    o_ref[...] = out
    ob_ref[...] = out.astype(BF16)


def _moe_combine(y_tok, w_tok, x, xb, wg, wu, wd, g, b, first_row, prev, tm=256):
    n, d = x.shape
    n_part = y_tok.shape[1]
    f = wg.shape[1]
    tm = _tile(n_part, tm)
    assert first_row % tm == 0
    first = first_row // tm
    rows = lambda width: pl.BlockSpec((tm, width), lambda i: (i + first, 0))
    extra, extra_specs, aliases = _carry_output(prev, 9)
    return pl.pallas_call(
        _combine_kernel,
        grid=(n_part // tm,),
        in_specs=[pl.BlockSpec((TOP_K, tm, d), lambda i: (0, i, 0)), rows(TOP_K), rows(d), rows(d),
                  _const((d, f)), _const((d, f)), _const((f, d)), _const((1, d)), _const((1, d))]
        + extra_specs,
        out_specs=[rows(d), rows(d)],
        out_shape=[jax.ShapeDtypeStruct((n, d), F32), jax.ShapeDtypeStruct((n, d), BF16)],
        input_output_aliases=aliases,
        compiler_params=_params("parallel"),
    )(y_tok, w_tok, x, xb, wg, wu, wd, g.reshape(1, d), b.reshape(1, d), *extra)


def _grouped_kernel(blk_e_ref, nused_ref, x_ref, wg_ref, wu_ref, wd_ref, *rest, first_blk):
    o_ref, wg16, wu16, wd16 = rest[-4:]
    i = pl.program_id(0)
    blk = i + first_blk
    active = blk < nused_ref[0]
    new_expert = (i == 0) | (blk_e_ref[blk] != blk_e_ref[jnp.maximum(blk - 1, 0)])

    @pl.when(active & new_expert)
    def _():
        wg16[...] = wg_ref[0].astype(BF16)
        wu16[...] = wu_ref[0].astype(BF16)
        wd16[...] = wd_ref[0].astype(BF16)

    @pl.when(active)
    def _():
        o_ref[...] = _swiglu_body(x_ref[...], wg16[...], wu16[...], wd16[...]).astype(o_ref.dtype)


def _grouped_experts(x_part, blk_e, nused, layer, wg, wu, wd, first_blk, total_rows, prev):
    p, d = x_part.shape
    f = wg.shape[3]
    nb = p // MOE_ROWS
    active = lambda i, nu: jnp.minimum(i, jnp.clip(nu[0] - first_blk - 1, 0, nb - 1))
    row_in = lambda i, be, nu: (active(i, nu), 0)
    row_out = lambda i, be, nu: (active(i, nu) + first_blk, 0)
    exp = lambda i, be, nu: (layer, be[active(i, nu) + first_blk], 0, 0)
    extra, extra_specs, aliases = _carry_output(None if prev is None else [prev], 6)
    return pl.pallas_call(
        functools.partial(_grouped_kernel, first_blk=first_blk),
        grid_spec=pltpu.PrefetchScalarGridSpec(
            num_scalar_prefetch=2,
            grid=(nb,),
            in_specs=[pl.BlockSpec((MOE_ROWS, d), row_in),
                      pl.BlockSpec((None, 1, d, f), exp),
                      pl.BlockSpec((None, 1, d, f), exp),
                      pl.BlockSpec((None, 1, f, d), exp)] + extra_specs,
            out_specs=pl.BlockSpec((MOE_ROWS, d), row_out),
            scratch_shapes=[pltpu.VMEM((d, f), BF16), pltpu.VMEM((d, f), BF16), pltpu.VMEM((f, d), BF16)]),
        out_shape=jax.ShapeDtypeStruct((total_rows, d), BF16),
        input_output_aliases=aliases,
        compiler_params=_params("arbitrary"),
    )(blk_e, nused, x_part, wg, wu, wd, *extra)


def _moe_block(x, xb, layer, w_router, router_bias, w_gate, w_up, w_down, ws_gate, ws_up, ws_down,
               ln_g, ln_b):
    n, d = x.shape
    tm = TOKEN_TILE
    top_e, top_w, tile_cnt = _route(x, w_router, router_bias, tm)
    tile_cnt = tile_cnt.astype(jnp.int32)
    counts = jnp.sum(tile_cnt, axis=0)
    padded = (counts + MOE_ROWS - 1) // MOE_ROWS * MOE_ROWS
    pend = jnp.cumsum(padded)
    pstart = pend - padded
    tile_base = pstart[None, :] + jnp.cumsum(tile_cnt, axis=0) - tile_cnt
    dest = _dest_rows(top_e, tile_base, tm)
    nblk = (n * TOP_K) // MOE_ROWS + N_EXPERTS
    blk_e = jnp.minimum(jnp.searchsorted(pend, jnp.arange(nblk) * MOE_ROWS, side='right'),
                        N_EXPERTS - 1).astype(jnp.int32)
    nused = (pend[-1] // MOE_ROWS).astype(jnp.int32).reshape(1)
    tok = jnp.broadcast_to(jnp.arange(n, dtype=jnp.int32)[None, :], (TOP_K, n))
    total_rows = nblk * MOE_ROWS
    pad_tok = jnp.arange(total_rows, dtype=jnp.int32) % n
    slot_tok = pad_tok.at[dest.reshape(-1)].set(tok.reshape(-1), unique_indices=True,
                                                mode="promise_in_bounds")
    assert nblk % MOE_PARTS == 0 and n % MOE_PARTS == 0
    wg, wu, wd = w_gate, w_up, w_down
    part_blks = nblk // MOE_PARTS
    part_rows = part_blks * MOE_ROWS
    y_sorted = None
    for j in range(MOE_PARTS):
        x_part = xb.at[slot_tok[j * part_rows:(j + 1) * part_rows]].get(mode="promise_in_bounds")
        y_sorted = _grouped_experts(x_part, blk_e, nused, layer, wg, wu, wd, j * part_blks, total_rows,
                                    y_sorted)
    w_tok = top_w.T
    sg, su, sd = ws_gate.astype(BF16), ws_up.astype(BF16), ws_down.astype(BF16)
    part_tok = n // MOE_PARTS
    out = None
    for j in range(MOE_PARTS):
        rows = dest[:, j * part_tok:(j + 1) * part_tok].reshape(-1)
        y_tok = y_sorted.at[rows].get(mode="promise_in_bounds").reshape(TOP_K, part_tok, d)
        out = _moe_combine(y_tok, w_tok, x, xb, sg, su, sd, ln_g, ln_b, j * part_tok, out)
    return out


def _per_group(shapes, fn):
    out = None
    tok = 0
    for (b, t) in shapes:
        assert tok % t == 0
        out = fn(b, t, tok // t, out)
        tok += b * t
    return out


def _even_mixer(x, xb, shapes, tables, w_in, lb, hgrn_norm_w, q_norm_w, k_norm_w, w_out, ln_g, ln_b):
    ha, q, k, v = _even_prep(xb, shapes, w_in, q_norm_w, k_norm_w, tables)
    lb2 = lb.reshape(2, A_WIDTH)
    o_f, o_b = _per_group(shapes, lambda b, t, s0, prev: _hgrn_scan(ha, lb2, b, t, s0, prev))
    att = _per_group(shapes, lambda b, t, s0, prev: _attention(
        q, k, v, b, t, s0, prev, kv_heads=B_KV_HEADS, groups=B_GROUP, dq=B_HEAD_DIM, dv=B_HEAD_DIM))
    return _even_out(o_f, o_b, ha, att, hgrn_norm_w, w_out, x, ln_g, ln_b)


def _mla_mixer(x, xb, shapes, tables, w_in, q_a_norm_w, w_q_b, kv_a_norm_w, w_kv_b, w_out, ln_g, ln_b):
    q, k, v = _mla_prep(xb, shapes, w_in, q_a_norm_w, w_q_b, kv_a_norm_w, w_kv_b, tables)
    att = _per_group(shapes, lambda b, t, s0, prev: _attention(
        q, k, v, b, t, s0, prev, kv_heads=C_HEADS, groups=1, dq=C_QK_PAD, dv=C_V))
    return _matmul_residual_ln(att, w_out.astype(BF16), x, ln_g, ln_b)


def _trunks(xs, params):
    (w_in_even, lb_logits, hgrn_norm_w, q_norm_w, k_norm_w, w_out_even,
     w_in_odd, q_a_norm_w, w_q_b, kv_a_norm_w, w_kv_b, w_out_odd,
     w_router, router_bias, w_gate, w_up, w_down, ws_gate, ws_up, ws_down,
     ln_mix_g, ln_mix_b, ln_ffn_g, ln_ffn_b) = params
    shapes = [(x.shape[0], x.shape[1]) for x in xs]
    for (b, t) in shapes:
        assert t % TOKEN_TILE == 0 and t % GRID_W == 0
    d = xs[0].shape[-1]
    x = jnp.concatenate([a.reshape(-1, d) for a in xs], axis=0)
    xb = x.astype(BF16)
    t_max = max(t for _, t in shapes)
    tables_b = _rope_cs(t_max, B_HEAD_DIM)
    tables_c = _rope_cs(t_max, C_ROPE)
    lb_all = jnp.cumsum(jax.nn.softmax(lb_logits.astype(F32), axis=0), axis=0)
    for l in range(DEPTH):
        j = l // 2
        if l % 2 == 0:
            x, xb = _even_mixer(x, xb, shapes, tables_b, w_in_even[j], lb_all[j], hgrn_norm_w[j],
                                q_norm_w[j], k_norm_w[j], w_out_even[j], ln_mix_g[l], ln_mix_b[l])
        else:
            x, xb = _mla_mixer(x, xb, shapes, tables_c, w_in_odd[j], q_a_norm_w[j], w_q_b[j],
                               kv_a_norm_w[j], w_kv_b[j], w_out_odd[j], ln_mix_g[l], ln_mix_b[l])
        x, xb = _moe_block(x, xb, l, w_router[l], router_bias[l], w_gate, w_up, w_down,
                           ws_gate[l], ws_up[l], ws_down[l], ln_ffn_g[l], ln_ffn_b[l])
    outs = []
    off = 0
    for (b, t) in shapes:
        outs.append(x[off:off + b * t].reshape(b, t, d))
        off += b * t
    return tuple(outs)


def kernel(x_prompt, x_sample, w_in_even, lb_logits, hgrn_norm_w, q_norm_w, k_norm_w, w_out_even,
           w_in_odd, q_a_norm_w, w_q_b, kv_a_norm_w, w_kv_b, w_out_odd,
           w_router, router_bias, w_gate, w_up, w_down, ws_gate, ws_up, ws_down,
           ln_mix_g, ln_mix_b, ln_ffn_g, ln_ffn_b):
    params = (w_in_even, lb_logits, hgrn_norm_w, q_norm_w, k_norm_w, w_out_even,
              w_in_odd, q_a_norm_w, w_q_b, kv_a_norm_w, w_kv_b, w_out_odd,
              w_router, router_bias, w_gate, w_up, w_down, ws_gate, ws_up, ws_down,
              ln_mix_g, ln_mix_b, ln_ffn_g, ln_ffn_b)
    return _trunks([x_prompt, x_sample], params)
```

```python
import functools

import jax
import jax.numpy as jnp
from jax import lax
from jax.experimental import pallas as pl
from jax.experimental.pallas import tpu as pltpu

F32 = jnp.float32
BF16 = jnp.bfloat16

GRID_W = 64
ROPE_THETA = 10000.0
RMS_EPS = 1e-6
LN_EPS = 1e-5
DEPTH = 2
ALPHA = (2 * DEPTH) ** 0.25

A_HEADS = 4
A_DIM = 128
A_WIDTH = A_HEADS * A_DIM
A_COLS = 5 * A_WIDTH
B_HEADS = 4
B_KV_HEADS = 2
B_GROUP = B_HEADS // B_KV_HEADS
B_HEAD_DIM = 128
B_WIDTH = B_HEADS * B_HEAD_DIM
B_KVW = B_KV_HEADS * B_HEAD_DIM
C_HEADS = 8
C_NOPE = 128
C_ROPE = 64
C_V = 128
C_Q_LORA = 384
C_KV_LORA = 256
C_QK_PAD = 256
N_EXPERTS = 256
TOP_K = 8
N_GROUPS = 8
TOPK_GROUPS = 4
ROUTED_SCALE = 2.5

V7X_VMEM_LIMIT_BYTES = 56 * 1024 * 1024
LANES = 128
SUBLANES = 8

TOKEN_TILE = 512
GLA_CHUNK = 64
GLA_SUB = 16
MOE_ROWS = 512
MOE_PARTS = 4
NEG_BIG = -1e30
LOG2_E = 1.4426950408889634

_NT = (((1,), (1,)), ((), ()))


def _params(*semantics):
    return pltpu.CompilerParams(dimension_semantics=semantics,
                                vmem_limit_bytes=V7X_VMEM_LIMIT_BYTES)


def _tile(n, want):
    t = min(n, want)
    assert n % t == 0, (n, t)
    return t


def _rows(tm, width):
    return pl.BlockSpec((tm, width), lambda i: (i, 0))


def _const(shape):
    return pl.BlockSpec(shape, lambda i: (0,) * len(shape))


def _layer_norm(x, g, b):
    mu = jnp.mean(x, axis=-1, keepdims=True)
    xc = x - mu
    var = jnp.mean(xc * xc, axis=-1, keepdims=True)
    return xc * lax.rsqrt(var + LN_EPS) * g + b


def _inv_rms(x):
    return lax.rsqrt(jnp.mean(x * x, axis=-1, keepdims=True) + RMS_EPS)


def _rope_swap(n):
    q = n // 4
    a = jnp.arange(q)
    return jnp.concatenate([a + q, a, a + 3 * q, a + 2 * q])


def _rope_cs(t, rot_dim):
    row = (jnp.arange(t) // GRID_W).astype(F32)
    col = (jnp.arange(t) % GRID_W).astype(F32)
    half = rot_dim // 2
    inv = ROPE_THETA ** (-jnp.arange(0, half, 2, dtype=F32) / half)
    ar = row[:, None] * inv[None, :]
    ac = col[:, None] * inv[None, :]
    c = jnp.concatenate([jnp.cos(ar), jnp.cos(ar), jnp.cos(ac), jnp.cos(ac)], axis=1)
    s = jnp.concatenate([-jnp.sin(ar), jnp.sin(ar), -jnp.sin(ac), jnp.sin(ac)], axis=1)
    pad = ((0, 0), (0, LANES - rot_dim))
    return jnp.pad(c, pad), jnp.pad(s, pad)


def _pos_block(shapes, tm):
    groups = []
    first = 0
    for (b, t) in shapes:
        groups.append((first, t // tm))
        first += b * t // tm

    def index_map(i):
        first_tile, per_seq = groups[0]
        blk = (i - first_tile) % per_seq
        for first_tile, per_seq in groups[1:]:
            blk = jnp.where(i >= first_tile, (i - first_tile) % per_seq, blk)
        return (blk, 0)

    return index_map


def _even_prep_kernel(x_ref, w_ref, qw_ref, kw_ref, c_ref, s_ref, ha_ref, q_ref, k_ref, v_ref):
    xb = x_ref[...]
    for j in range(0, A_COLS, A_WIDTH):
        ha_ref[:, j:j + A_WIDTH] = jnp.dot(xb, w_ref[:, j:j + A_WIDTH], preferred_element_type=F32)
    c = c_ref[...]
    s = s_ref[...]
    q0, k0, v0 = A_COLS, A_COLS + B_WIDTH, A_COLS + B_WIDTH + B_KVW
    qs0 = v0 + B_KVW
    ks0 = qs0 + B_WIDTH
    d = B_HEAD_DIM

    def normed_rope(col, col_sw, heads, w2_ref, scale, o_ref):
        y = jnp.dot(xb, w_ref[:, col:col + heads * d], preferred_element_type=F32)
        ysw = jnp.dot(xb, w_ref[:, col_sw:col_sw + heads * d], preferred_element_type=F32)
        for h in range(heads):
            sl = slice(h * d, (h + 1) * d)
            yh = y[:, sl]
            rot = yh * (w2_ref[0:1] * c) + ysw[:, sl] * (w2_ref[1:2] * s)
            o_ref[:, sl] = (rot * (_inv_rms(yh) * scale)).astype(BF16)

    normed_rope(q0, qs0, B_HEADS, qw_ref, B_HEAD_DIM ** -0.5 * LOG2_E, q_ref)
    normed_rope(k0, ks0, B_KV_HEADS, kw_ref, 1.0, k_ref)
    v_ref[...] = jnp.dot(xb, w_ref[:, v0:v0 + B_KVW], preferred_element_type=F32).astype(BF16)


def _even_prep(xb, shapes, w_in, q_norm_w, k_norm_w, tables):
    n, dm = xb.shape
    tm = TOKEN_TILE
    swap = _rope_swap(B_HEAD_DIM)
    q0, k0 = A_COLS, A_COLS + B_WIDTH
    wq = w_in[:, q0:q0 + B_WIDTH].reshape(dm, B_HEADS, B_HEAD_DIM)[:, :, swap].reshape(dm, B_WIDTH)
    wk = w_in[:, k0:k0 + B_KVW].reshape(dm, B_KV_HEADS, B_HEAD_DIM)[:, :, swap].reshape(dm, B_KVW)
    w_ext = jnp.concatenate([w_in, wq, wk], axis=1).astype(BF16)
    qw = jnp.stack([q_norm_w, q_norm_w[swap]]).astype(F32)
    kw = jnp.stack([k_norm_w, k_norm_w[swap]]).astype(F32)
    pos = pl.BlockSpec((tm, LANES), _pos_block(shapes, tm))
    return pl.pallas_call(
        _even_prep_kernel,
        grid=(n // tm,),
        in_specs=[_rows(tm, dm), _const(w_ext.shape), _const((2, B_HEAD_DIM)), _const((2, B_HEAD_DIM)),
                  pos, pos],
        out_specs=[_rows(tm, A_COLS), _rows(tm, B_WIDTH), _rows(tm, B_KVW), _rows(tm, B_KVW)],
        out_shape=[jax.ShapeDtypeStruct((n, A_COLS), F32), jax.ShapeDtypeStruct((n, B_WIDTH), BF16),
                   jax.ShapeDtypeStruct((n, B_KVW), BF16), jax.ShapeDtypeStruct((n, B_KVW), BF16)],
        compiler_params=_params("parallel"),
    )(xb, w_ext, qw, kw, tables[0], tables[1])


def _even_out_kernel(of_ref, ob_ref, g_ref, att_ref, nw_ref, w_ref, r_ref, lg_ref, lb_ref, o_ref, ob16_ref):
    acc = ALPHA * r_ref[...] + jnp.dot(att_ref[...], w_ref[A_WIDTH:, :], preferred_element_type=F32)
    nw = nw_ref[...]
    parts = []
    for h in range(A_HEADS):
        sl = slice(h * A_DIM, (h + 1) * A_DIM)
        o = of_ref[:, sl] + ob_ref[:, sl]
        g = g_ref[:, sl]
        parts.append((o * _inv_rms(o) * nw * (g * jax.nn.sigmoid(g))).astype(BF16))
    acc += jnp.dot(jnp.concatenate(parts, axis=1), w_ref[:A_WIDTH, :], preferred_element_type=F32)
    out = _layer_norm(acc, lg_ref[...], lb_ref[...])
    o_ref[...] = out
    ob16_ref[...] = out.astype(BF16)


def _even_out(o_f, o_b, ha, att, hgrn_norm_w, w_out, x, ln_g, ln_b):
    n, d = x.shape
    tm = TOKEN_TILE
    gate = pl.BlockSpec((tm, A_WIDTH), lambda i: (i, 4))
    return pl.pallas_call(
        _even_out_kernel,
        grid=(n // tm,),
        in_specs=[_rows(tm, A_WIDTH), _rows(tm, A_WIDTH), gate, _rows(tm, B_WIDTH), _const((1, A_DIM)),
                  _const(w_out.shape), _rows(tm, d), _const((1, d)), _const((1, d))],
        out_specs=[_rows(tm, d), _rows(tm, d)],
        out_shape=[jax.ShapeDtypeStruct((n, d), F32), jax.ShapeDtypeStruct((n, d), BF16)],
        compiler_params=_params("parallel"),
    )(o_f, o_b, ha, att, hgrn_norm_w.reshape(1, A_DIM).astype(F32), w_out.astype(BF16), x,
      ln_g.reshape(1, d), ln_b.reshape(1, d))


def _mla_prep_kernel(x_ref, win_ref, qn_ref, kn_ref, wq_ref, wkv_ref, c_ref, s_ref, q_ref, k_ref, v_ref):
    xb = x_ref[...]
    c = c_ref[...]
    s = s_ref[...]
    hc = jnp.dot(xb, win_ref[...], preferred_element_type=F32)
    cq = hc[:, :C_Q_LORA]
    cq = (cq * _inv_rms(cq) * qn_ref[...]).astype(BF16)
    ckv = hc[:, C_Q_LORA:C_Q_LORA + C_KV_LORA]
    ckv = (ckv * _inv_rms(ckv) * kn_ref[...]).astype(BF16)
    kr0 = C_Q_LORA + C_KV_LORA
    k_rope = (hc[:, kr0:kr0 + LANES] * c + hc[:, kr0 + LANES:kr0 + 2 * LANES] * s).astype(BF16)
    scale = (C_NOPE + C_ROPE) ** -0.5 * LOG2_E
    hw = C_HEADS * C_NOPE
    q_nope = jnp.dot(cq, wq_ref[:, :hw], preferred_element_type=F32)
    q_r = jnp.dot(cq, wq_ref[:, hw:2 * hw], preferred_element_type=F32)
    q_rs = jnp.dot(cq, wq_ref[:, 2 * hw:], preferred_element_type=F32)
    k_nope = jnp.dot(ckv, wkv_ref[:, :hw], preferred_element_type=F32)
    v_ref[...] = jnp.dot(ckv, wkv_ref[:, hw:], preferred_element_type=F32).astype(BF16)
    for h in range(C_HEADS):
        sl = slice(h * LANES, (h + 1) * LANES)
        lo = h * C_QK_PAD
        q_ref[:, lo:lo + LANES] = (q_nope[:, sl] * scale).astype(BF16)
        q_ref[:, lo + LANES:lo + 2 * LANES] = ((q_r[:, sl] * c + q_rs[:, sl] * s) * scale).astype(BF16)
        k_ref[:, lo:lo + LANES] = k_nope[:, sl].astype(BF16)
        k_ref[:, lo + LANES:lo + 2 * LANES] = k_rope


def _mla_prep(xb, shapes, w_in, q_a_norm_w, w_q_b, kv_a_norm_w, w_kv_b, tables):
    n, dm = xb.shape
    tm = TOKEN_TILE
    swap = _rope_swap(C_ROPE)
    lane_pad = lambda w: jnp.pad(w, ((0, 0),) * (w.ndim - 1) + ((0, LANES - C_ROPE),))
    kr0 = C_Q_LORA + C_KV_LORA
    w_kr = w_in[:, kr0:]
    win = jnp.concatenate([w_in[:, :kr0], lane_pad(w_kr), lane_pad(w_kr[:, swap])], axis=1).astype(BF16)
    wq = w_q_b.reshape(C_Q_LORA, C_HEADS, C_NOPE + C_ROPE)
    wq_r = wq[:, :, C_NOPE:]
    hw = C_HEADS * LANES
    wq_all = jnp.concatenate([wq[:, :, :C_NOPE].reshape(C_Q_LORA, hw),
                              lane_pad(wq_r).reshape(C_Q_LORA, hw),
                              lane_pad(wq_r[:, :, swap]).reshape(C_Q_LORA, hw)], axis=1).astype(BF16)
    wkv = w_kv_b.reshape(C_KV_LORA, C_HEADS, C_NOPE + C_V)
    wkv_all = jnp.concatenate([wkv[:, :, :C_NOPE].reshape(C_KV_LORA, hw),
                               wkv[:, :, C_NOPE:].reshape(C_KV_LORA, hw)], axis=1).astype(BF16)
    pos = pl.BlockSpec((tm, LANES), _pos_block(shapes, tm))
    qk = jax.ShapeDtypeStruct((n, C_HEADS * C_QK_PAD), BF16)
    return pl.pallas_call(
        _mla_prep_kernel,
        grid=(n // tm,),
        in_specs=[_rows(tm, dm), _const(win.shape), _const((1, C_Q_LORA)), _const((1, C_KV_LORA)),
                  _const(wq_all.shape), _const(wkv_all.shape), pos, pos],
        out_specs=[_rows(tm, C_HEADS * C_QK_PAD), _rows(tm, C_HEADS * C_QK_PAD), _rows(tm, C_HEADS * C_V)],
        out_shape=[qk, qk, jax.ShapeDtypeStruct((n, C_HEADS * C_V), BF16)],
        compiler_params=_params("parallel"),
    )(xb, win, q_a_norm_w.reshape(1, -1).astype(F32), kv_a_norm_w.reshape(1, -1).astype(F32),
      wq_all, wkv_all, tables[0], tables[1])


def _mm_ln_kernel(x_ref, w_ref, r_ref, g_ref, b_ref, o_ref, ob_ref):
    y = ALPHA * r_ref[...] + jnp.dot(x_ref[...], w_ref[...], preferred_element_type=F32)
    out = _layer_norm(y, g_ref[...], b_ref[...])
    o_ref[...] = out
    ob_ref[...] = out.astype(BF16)


def _matmul_residual_ln(x, w, resid, g, b):
    m, k = x.shape
    d = w.shape[1]
    tm = TOKEN_TILE
    return pl.pallas_call(
        _mm_ln_kernel,
        grid=(m // tm,),
        in_specs=[_rows(tm, k), _const((k, d)), _rows(tm, d), _const((1, d)), _const((1, d))],
        out_specs=[_rows(tm, d), _rows(tm, d)],
        out_shape=[jax.ShapeDtypeStruct((m, d), F32), jax.ShapeDtypeStruct((m, d), BF16)],
        compiler_params=_params("parallel"),
    )(x, w, resid, g.reshape(1, d), b.reshape(1, d))


def _row_iota(rows, cols):
    return lax.broadcasted_iota(jnp.int32, (rows, cols), 0).astype(F32)


def _first_max(cur, iota, n):
    m = jnp.max(cur, axis=0, keepdims=True)
    idx = jnp.min(jnp.where(cur == m, iota, float(n)), axis=0, keepdims=True)
    return m, idx


def _route_kernel(x_ref, wh_ref, wl_ref, bias_ref, e_ref, w_ref, cnt_ref):
    x = x_ref[...]
    tm = x.shape[0]
    xh = x.astype(BF16)
    xl = (x - xh.astype(F32)).astype(BF16)
    wh = wh_ref[...]
    logits = lax.dot_general(wh, xh, _NT, preferred_element_type=F32)
    logits += lax.dot_general(wh, xl, _NT, preferred_element_type=F32)
    logits += lax.dot_general(wl_ref[...], xh, _NT, preferred_element_type=F32)
    scores = jax.nn.sigmoid(logits)
    choice = scores + bias_ref[...]
    gsz = N_EXPERTS // N_GROUPS
    sub_iota = _row_iota(gsz, tm)
    group_rows = []
    for g in range(N_GROUPS):
        grp = choice[g * gsz:(g + 1) * gsz]
        m1, first = _first_max(grp, sub_iota, gsz)
        m2 = jnp.max(jnp.where(sub_iota == first, -jnp.inf, grp), axis=0, keepdims=True)
        group_rows.append(m1 + m2)
    cur = jnp.concatenate(group_rows, axis=0)
    g_iota = _row_iota(N_GROUPS, tm)
    sel = jnp.zeros((N_GROUPS, tm), F32)
    for _ in range(TOPK_GROUPS):
        _, gi = _first_max(cur, g_iota, N_GROUPS)
        pick = g_iota == gi
        sel = jnp.where(pick, 1.0, sel)
        cur = jnp.where(pick, -jnp.inf, cur)
    masked = jnp.concatenate(
        [jnp.where(sel[g:g + 1] > 0.5, choice[g * gsz:(g + 1) * gsz], -jnp.inf)
         for g in range(N_GROUPS)], axis=0)
    e_iota = _row_iota(N_EXPERTS, tm)
    cur = masked
    ids, vals = [], []
    for _ in range(TOP_K):
        _, ei = _first_max(cur, e_iota, N_EXPERTS)
        pick = e_iota == ei
        vals.append(jnp.sum(jnp.where(pick, scores, 0.0), axis=0, keepdims=True))
        ids.append(ei)
        cur = jnp.where(pick, -jnp.inf, cur)
    member = jnp.where((masked > -jnp.inf) & (cur == -jnp.inf), 1.0, 0.0).astype(BF16)
    s = jnp.concatenate(vals, axis=0)
    e_ref[...] = jnp.concatenate(ids, axis=0).astype(jnp.int32)
    w_ref[...] = s / (jnp.sum(s, axis=0, keepdims=True) + 1e-20) * ROUTED_SCALE
    cnt_ref[...] = lax.dot_general(jnp.ones((8, tm), BF16), member, _NT, preferred_element_type=F32)


def _route(x, w_router, router_bias, tm):
    n, d = x.shape
    wt = w_router.T
    wh = wt.astype(BF16)
    wl = (wt - wh.astype(F32)).astype(BF16)
    e, w, cnt = pl.pallas_call(
        _route_kernel,
        grid=(n // tm,),
        in_specs=[_rows(tm, d), _const((N_EXPERTS, d)), _const((N_EXPERTS, d)), _const((N_EXPERTS, 1))],
        out_specs=[pl.BlockSpec((TOP_K, tm), lambda i: (0, i)),
                   pl.BlockSpec((TOP_K, tm), lambda i: (0, i)),
                   pl.BlockSpec((None, 8, N_EXPERTS), lambda i: (i, 0, 0))],
        out_shape=[jax.ShapeDtypeStruct((TOP_K, n), jnp.int32),
                   jax.ShapeDtypeStruct((TOP_K, n), F32),
                   jax.ShapeDtypeStruct((n // tm, 8, N_EXPERTS), F32)],
        compiler_params=_params("parallel"),
    )(x, wh, wl, router_bias.reshape(N_EXPERTS, 1).astype(F32))
    return e, w, cnt[:, 0, :]


def _dest_kernel(e_ref, base_ref, d_ref):
    e = e_ref[...].astype(F32)
    tm = e.shape[1]
    e_iota = _row_iota(N_EXPERTS, tm)
    picks = [e_iota == e[k:k + 1] for k in range(TOP_K)]
    member = jnp.zeros((N_EXPERTS, tm), F32)
    for p in picks:
        member = jnp.where(p, 1.0, member)
    r = lax.broadcasted_iota(jnp.int32, (tm, tm), 0)
    c = lax.broadcasted_iota(jnp.int32, (tm, tm), 1)
    upper = jnp.where(r <= c, 1.0, 0.0).astype(BF16)
    prefix = jnp.dot(member.astype(BF16), upper, preferred_element_type=F32)
    val = prefix + (base_ref[...] - 1.0)
    rows = [jnp.sum(jnp.where(p, val, 0.0), axis=0, keepdims=True) for p in picks]
    d_ref[...] = jnp.concatenate(rows, axis=0).astype(jnp.int32)


def _dest_rows(top_e, tile_base, tm):
    k, n = top_e.shape
    return pl.pallas_call(
        _dest_kernel,
        grid=(n // tm,),
        in_specs=[pl.BlockSpec((k, tm), lambda i: (0, i)),
                  pl.BlockSpec((None, N_EXPERTS, 1), lambda i: (i, 0, 0))],
        out_specs=pl.BlockSpec((k, tm), lambda i: (0, i)),
        out_shape=jax.ShapeDtypeStruct((k, n), jnp.int32),
        compiler_params=_params("parallel"),
    )(top_e, tile_base.astype(F32)[:, :, None])


def _flash_kernel(q_ref, k_ref, v_ref, *rest, groups, dq, dv, tk, row_parts):
    o_ref = rest[-1]
    tq = q_ref.shape[0]
    t = k_ref.shape[0]
    rq = tq // row_parts
    chains = [(g, r * rq) for g in range(groups) for r in range(row_parts)]
    qs = [q_ref[r0:r0 + rq, g * dq:(g + 1) * dq] for g, r0 in chains]

    def body(c, carry):
        start = pl.multiple_of(c * tk, tk)
        k = k_ref[pl.ds(start, tk), :]
        v = v_ref[pl.ds(start, tk), :]
        out = []
        for n in range(len(chains)):
            m, l, acc = carry[n]
            s = lax.dot_general(qs[n], k, _NT, preferred_element_type=F32)
            m_new = jnp.maximum(m, jnp.max(s, axis=-1, keepdims=True))
            alpha = jnp.exp2(m - m_new)
            p = jnp.exp2(s - m_new)
            l = alpha * l + jnp.sum(p, axis=-1, keepdims=True)
            acc = alpha * acc + jnp.dot(p.astype(BF16), v, preferred_element_type=F32)
            out.append((m_new, l, acc))
        return tuple(out)

    init = tuple((jnp.full((rq, 1), NEG_BIG, F32), jnp.zeros((rq, 1), F32),
                  jnp.zeros((rq, dv), F32)) for _ in chains)
    res = lax.fori_loop(0, t // tk, body, init, unroll=2)
    for n, (g, r0) in enumerate(chains):
        _, l, acc = res[n]
        o_ref[r0:r0 + rq, g * dv:(g + 1) * dv] = (acc / l).astype(o_ref.dtype)


def _seq_view(a, t):
    return a.reshape(a.shape[0] // t, t, a.shape[1])


def _carry_output(prev, n_in):
    if prev is None:
        return [], [], {}
    prev = list(prev)
    return prev, [pl.BlockSpec(memory_space=pl.ANY)] * len(prev), {n_in + k: k for k in range(len(prev))}


def _attention(q, k, v, seqs, t, first_seq, prev, *, kv_heads, groups, dq, dv, row_parts, tq=1024, tk=1024):
    n = q.shape[0]
    tq = _tile(t, tq)
    tk = _tile(t, tk)
    kern = functools.partial(_flash_kernel, groups=groups, dq=dq, dv=dv, tk=tk, row_parts=row_parts)
    extra, extra_specs, aliases = _carry_output(None if prev is None else [_seq_view(prev, t)], 3)
    out = pl.pallas_call(
        kern,
        grid=(seqs, kv_heads, t // tq),
        in_specs=[pl.BlockSpec((None, tq, groups * dq), lambda i, h, j: (i + first_seq, j, h)),
                  pl.BlockSpec((None, t, dq), lambda i, h, j: (i + first_seq, 0, h)),
                  pl.BlockSpec((None, t, dv), lambda i, h, j: (i + first_seq, 0, h))] + extra_specs,
        out_specs=pl.BlockSpec((None, tq, groups * dv), lambda i, h, j: (i + first_seq, j, h)),
        out_shape=jax.ShapeDtypeStruct((n // t, t, kv_heads * groups * dv), BF16),
        input_output_aliases=aliases,
        compiler_params=_params("parallel", "parallel", "arbitrary"),
    )(_seq_view(q, t), _seq_view(k, t), _seq_view(v, t), *extra)
    return out.reshape(n, -1)


def _cumsum_time(g, rev):
    c = g.shape[0]
    row = lax.broadcasted_iota(jnp.int32, (c, c), 0)
    col = lax.broadcasted_iota(jnp.int32, (c, c), 1)
    tri = jnp.where((col >= row) if rev else (col <= row), 1.0, 0.0).astype(BF16)
    g1 = g.astype(BF16)
    r1 = g - g1.astype(F32)
    g2 = r1.astype(BF16)
    g3 = (r1 - g2.astype(F32)).astype(BF16)
    out = jnp.dot(tri, g1, preferred_element_type=F32)
    out += jnp.dot(tri, g2, preferred_element_type=F32)
    out += jnp.dot(tri, g3, preferred_element_type=F32)
    return out


def _gla_gates(fl, lb, rev):
    fg = lb + (1.0 - lb) * jax.nn.sigmoid(fl)
    return 1.0 - fg, _cumsum_time(jnp.log2(fg), rev)


def _gla_state(q, v, kk, b, st, rev):
    b_edge = b[0:1] if rev else b[GLA_CHUNK - 1:GLA_CHUNK]
    o = lax.dot_general((q * jnp.exp2(b)).astype(BF16), st.astype(BF16), _NT,
                        preferred_element_type=F32)
    kdec = (kk * jnp.exp2(b_edge - b)).astype(BF16)
    st_new = st * jnp.exp2(b_edge) + lax.dot_general(
        v.astype(BF16), kdec, (((0,), (0,)), ((), ())), preferred_element_type=F32)
    return o, st_new


def _gla_earlier_blocks(i, q, v, kk, b, rev):
    cn, sub = GLA_CHUNK, GLA_SUB
    lo, hi = i * sub, (i + 1) * sub
    if rev:
        if hi == cn:
            return None
        ref_b, plo, phi = b[hi:hi + 1], hi, cn
    else:
        if lo == 0:
            return None
        ref_b, plo, phi = b[lo - 1:lo], 0, lo
    qd = (q[lo:hi] * jnp.exp2(b[lo:hi] - ref_b)).astype(BF16)
    kd = (kk[plo:phi] * jnp.exp2(ref_b - b[plo:phi])).astype(BF16)
    a = lax.dot_general(qd, kd, _NT, preferred_element_type=F32)
    return jnp.dot(a.astype(BF16), v[plo:phi].astype(BF16), preferred_element_type=F32)


def _gla_diagonal(i, q, v, kk, b, ones, rev):
    sub = GLA_SUB
    lo = i * sub
    bi, qi, ki, vi = b[lo:lo + sub], q[lo:lo + sub], kk[lo:lo + sub], v[lo:lo + sub]
    t_idx = lax.broadcasted_iota(jnp.int32, (SUBLANES, 1), 0)
    terms, plan = [], []
    for s in range(sub):
        for r0 in range(0, sub, SUBLANES):
            if (r0 > s) if rev else (r0 + SUBLANES - 1 < s):
                continue
            d = bi[r0:r0 + SUBLANES] - bi[s:s + 1]
            if r0 <= s < r0 + SUBLANES:
                keep = (t_idx + r0 <= s) if rev else (t_idx + r0 >= s)
                d = jnp.where(keep, d, NEG_BIG)
            terms.append(qi[r0:r0 + SUBLANES] * jnp.exp2(d) * ki[s:s + 1])
            plan.append((s, r0))
    a = jnp.dot(jnp.concatenate(terms, axis=0).astype(BF16), ones, preferred_element_type=F32)
    halves = {}
    for n, (s, r0) in enumerate(plan):
        term = a[n * SUBLANES:(n + 1) * SUBLANES] * vi[s:s + 1]
        halves[r0] = halves[r0] + term if r0 in halves else term
    return halves


def _hgrn_kernel(qf_ref, ff_ref, vf_ref, qb_ref, fb_ref, vb_ref, lb_ref, *rest):
    of_ref, ob_ref, st_ref = rest[-3:]

    @pl.when(pl.program_id(1) == 0)
    def _():
        st_ref[...] = jnp.zeros_like(st_ref)

    ones = jnp.ones((A_DIM, LANES), BF16)
    dirs = ((qf_ref, ff_ref, vf_ref, of_ref, False), (qb_ref, fb_ref, vb_ref, ob_ref, True))
    work = []
    for di, (q_ref, f_ref, v_ref, o_ref, rev) in enumerate(dirs):
        for h in range(A_HEADS):
            sl = slice(h * A_DIM, (h + 1) * A_DIM)
            kk, b = _gla_gates(f_ref[:, sl], lb_ref[di:di + 1, sl], rev)
            work.append(dict(di=di, h=h, sl=sl, rev=rev, o_ref=o_ref, q=q_ref[:, sl], v=v_ref[:, sl],
                             kk=kk, b=b))
    for w in work:
        w["o"], st_new = _gla_state(w["q"], w["v"], w["kk"], w["b"], st_ref[w["di"], w["h"]], w["rev"])
        st_ref[w["di"], w["h"]] = st_new
    for i in range(GLA_CHUNK // GLA_SUB):
        lo = i * GLA_SUB
        early = [_gla_earlier_blocks(i, w["q"], w["v"], w["kk"], w["b"], w["rev"]) for w in work]
        diag = [_gla_diagonal(i, w["q"], w["v"], w["kk"], w["b"], ones, w["rev"]) for w in work]
        for w, e, dg in zip(work, early, diag):
            oi = w["o"][lo:lo + GLA_SUB]
            if e is not None:
                oi = oi + e
            for r0, val in dg.items():
                w["o_ref"][lo + r0:lo + r0 + SUBLANES, w["sl"]] = oi[r0:r0 + SUBLANES] + val


def _hgrn_scan(ha, lb, seqs, t, first_seq, prev):
    n = ha.shape[0]
    nc = t // GLA_CHUNK
    blk = (None, GLA_CHUNK, A_WIDTH)
    fwd = lambda col: pl.BlockSpec(blk, lambda i, c: (i + first_seq, c, col))
    bwd = lambda col: pl.BlockSpec(blk, lambda i, c: (i + first_seq, nc - 1 - c, col))
    out = jax.ShapeDtypeStruct((n // t, t, A_WIDTH), F32)
    hv = _seq_view(ha, t)
    extra, extra_specs, aliases = _carry_output(
        None if prev is None else [_seq_view(p, t) for p in prev], 7)
    o_f, o_b = pl.pallas_call(
        _hgrn_kernel,
        grid=(seqs, nc),
        in_specs=[fwd(0), fwd(1), fwd(3), bwd(0), bwd(2), bwd(3),
                  pl.BlockSpec((2, A_WIDTH), lambda i, c: (0, 0))] + extra_specs,
        out_specs=[fwd(0), bwd(0)],
        out_shape=[out, out],
        scratch_shapes=[pltpu.VMEM((2, A_HEADS, A_DIM, A_DIM), F32)],
        input_output_aliases=aliases,
        compiler_params=_params("parallel", "arbitrary"),
    )(hv, hv, hv, hv, hv, hv, lb, *extra)
    return o_f.reshape(n, A_WIDTH), o_b.reshape(n, A_WIDTH)


def _swiglu_body(x, wg, wu, wd):
    g = jnp.dot(x, wg, preferred_element_type=F32)
    u = jnp.dot(x, wu, preferred_element_type=F32)
    hid = (g * jax.nn.sigmoid(g) * u).astype(BF16)
    return jnp.dot(hid, wd, preferred_element_type=F32)


def _combine_kernel(y_ref, w_ref, x_ref, xb_ref, wg_ref, wu_ref, wd_ref, g_ref, b_ref, *rest):
    o_ref, ob_ref = rest[-2:]
    w = w_ref[...]
    f = _swiglu_body(xb_ref[...], wg_ref[...], wu_ref[...], wd_ref[...])
    for k in range(TOP_K):
        f = f + y_ref[k].astype(F32) * w[:, k:k + 1]
    out = _layer_norm(ALPHA * x_ref[...] + f, g_ref[...], b_ref[...])
    o_ref[...] = out
    ob_ref[...] = out.astype(BF16)


def _moe_combine(y_tok, w_tok, x, xb, wg, wu, wd, g, b, first_row, prev, tm=256):
    n, d = x.shape
    n_part = y_tok.shape[1]
    f = wg.shape[1]
    tm = _tile(n_part, tm)
    assert first_row % tm == 0
    first = first_row // tm
    rows = lambda width: pl.BlockSpec((tm, width), lambda i: (i + first, 0))
    extra, extra_specs, aliases = _carry_output(prev, 9)
    return pl.pallas_call(
        _combine_kernel,
        grid=(n_part // tm,),
        in_specs=[pl.BlockSpec((TOP_K, tm, d), lambda i: (0, i, 0)), rows(TOP_K), rows(d), rows(d),
                  _const((d, f)), _const((d, f)), _const((f, d)), _const((1, d)), _const((1, d))]
        + extra_specs,
        out_specs=[rows(d), rows(d)],
        out_shape=[jax.ShapeDtypeStruct((n, d), F32), jax.ShapeDtypeStruct((n, d), BF16)],
        input_output_aliases=aliases,
        compiler_params=_params("parallel"),
    )(y_tok, w_tok, x, xb, wg, wu, wd, g.reshape(1, d), b.reshape(1, d), *extra)


def _grouped_kernel(blk_e_ref, nused_ref, x_ref, wg_ref, wu_ref, wd_ref, *rest, first_blk):
    o_ref, wg16, wu16, wd16 = rest[-4:]
    i = pl.program_id(0)
    blk = i + first_blk
    active = blk < nused_ref[0]
    new_expert = (i == 0) | (blk_e_ref[blk] != blk_e_ref[jnp.maximum(blk - 1, 0)])

    @pl.when(active & new_expert)
    def _():
        wg16[...] = wg_ref[0].astype(BF16)
        wu16[...] = wu_ref[0].astype(BF16)
        wd16[...] = wd_ref[0].astype(BF16)

    @pl.when(active)
    def _():
        o_ref[...] = _swiglu_body(x_ref[...], wg16[...], wu16[...], wd16[...]).astype(o_ref.dtype)


def _grouped_experts(x_part, blk_e, nused, layer, wg, wu, wd, first_blk, total_rows, prev):
    p, d = x_part.shape
    f = wg.shape[3]
    nb = p // MOE_ROWS
    active = lambda i, nu: jnp.minimum(i, jnp.clip(nu[0] - first_blk - 1, 0, nb - 1))
    row_in = lambda i, be, nu: (active(i, nu), 0)
    row_out = lambda i, be, nu: (active(i, nu) + first_blk, 0)
    exp = lambda i, be, nu: (layer, be[active(i, nu) + first_blk], 0, 0)
    extra, extra_specs, aliases = _carry_output(None if prev is None else [prev], 6)
    return pl.pallas_call(
        functools.partial(_grouped_kernel, first_blk=first_blk),
        grid_spec=pltpu.PrefetchScalarGridSpec(
            num_scalar_prefetch=2,
            grid=(nb,),
            in_specs=[pl.BlockSpec((MOE_ROWS, d), row_in),
                      pl.BlockSpec((None, 1, d, f), exp),
                      pl.BlockSpec((None, 1, d, f), exp),
                      pl.BlockSpec((None, 1, f, d), exp)] + extra_specs,
            out_specs=pl.BlockSpec((MOE_ROWS, d), row_out),
            scratch_shapes=[pltpu.VMEM((d, f), BF16), pltpu.VMEM((d, f), BF16), pltpu.VMEM((f, d), BF16)]),
        out_shape=jax.ShapeDtypeStruct((total_rows, d), BF16),
        input_output_aliases=aliases,
        compiler_params=_params("arbitrary"),
    )(blk_e, nused, x_part, wg, wu, wd, *extra)


def _moe_block(x, xb, layer, w_router, router_bias, w_gate, w_up, w_down, ws_gate, ws_up, ws_down,
               ln_g, ln_b):
    n, d = x.shape
    tm = TOKEN_TILE
    top_e, top_w, tile_cnt = _route(x, w_router, router_bias, tm)
    tile_cnt = tile_cnt.astype(jnp.int32)
    counts = jnp.sum(tile_cnt, axis=0)
    padded = (counts + MOE_ROWS - 1) // MOE_ROWS * MOE_ROWS
    pend = jnp.cumsum(padded)
    pstart = pend - padded
    tile_base = pstart[None, :] + jnp.cumsum(tile_cnt, axis=0) - tile_cnt
    dest = _dest_rows(top_e, tile_base, tm)
    nblk = (n * TOP_K) // MOE_ROWS + N_EXPERTS
    blk_e = jnp.minimum(jnp.searchsorted(pend, jnp.arange(nblk) * MOE_ROWS, side='right'),
                        N_EXPERTS - 1).astype(jnp.int32)
    nused = (pend[-1] // MOE_ROWS).astype(jnp.int32).reshape(1)
    tok = jnp.broadcast_to(jnp.arange(n, dtype=jnp.int32)[None, :], (TOP_K, n))
    total_rows = nblk * MOE_ROWS
    pad_tok = jnp.arange(total_rows, dtype=jnp.int32) % n
    filled = jnp.zeros((total_rows,), jnp.int32).at[dest.reshape(-1)].add(
        tok.reshape(-1) + 1, unique_indices=True, mode="promise_in_bounds")
    slot_tok = jnp.where(filled > 0, filled - 1, pad_tok)
    assert nblk % MOE_PARTS == 0 and n % MOE_PARTS == 0
    wg, wu, wd = w_gate, w_up, w_down
    part_blks = nblk // MOE_PARTS
    part_rows = part_blks * MOE_ROWS
    y_sorted = None
    for j in range(MOE_PARTS):
        x_part = xb.at[slot_tok[j * part_rows:(j + 1) * part_rows]].get(mode="promise_in_bounds")
        y_sorted = _grouped_experts(x_part, blk_e, nused, layer, wg, wu, wd, j * part_blks, total_rows,
                                    y_sorted)
    w_tok = top_w.T
    sg, su, sd = ws_gate.astype(BF16), ws_up.astype(BF16), ws_down.astype(BF16)
    part_tok = n // MOE_PARTS
    out = None
    for j in range(MOE_PARTS):
        rows = dest[:, j * part_tok:(j + 1) * part_tok].reshape(-1)
        y_tok = y_sorted.at[rows].get(mode="promise_in_bounds").reshape(TOP_K, part_tok, d)
        out = _moe_combine(y_tok, w_tok, x, xb, sg, su, sd, ln_g, ln_b, j * part_tok, out)
    return out


def _per_group(shapes, fn, init):
    out = init
    tok = 0
    for (b, t) in shapes:
        assert tok % t == 0
        out = fn(b, t, tok // t, out)
        tok += b * t
    return out


def _even_mixer(x, xb, shapes, tables, w_in, lb, hgrn_norm_w, q_norm_w, k_norm_w, w_out, ln_g, ln_b):
    ha, q, k, v = _even_prep(xb, shapes, w_in, q_norm_w, k_norm_w, tables)
    lb2 = lb.reshape(2, A_WIDTH)
    n = x.shape[0]
    o_f, o_b = _per_group(shapes, lambda b, t, s0, prev: _hgrn_scan(ha, lb2, b, t, s0, prev),
                          (jnp.zeros((n, A_WIDTH), F32), jnp.zeros((n, A_WIDTH), F32)))
    att = _per_group(shapes, lambda b, t, s0, prev: _attention(
        q, k, v, b, t, s0, prev, kv_heads=B_KV_HEADS, groups=B_GROUP, dq=B_HEAD_DIM, dv=B_HEAD_DIM,
        row_parts=1), jnp.zeros((n, B_WIDTH), BF16))
    return _even_out(o_f, o_b, ha, att, hgrn_norm_w, w_out, x, ln_g, ln_b)


def _mla_mixer(x, xb, shapes, tables, w_in, q_a_norm_w, w_q_b, kv_a_norm_w, w_kv_b, w_out, ln_g, ln_b):
    q, k, v = _mla_prep(xb, shapes, w_in, q_a_norm_w, w_q_b, kv_a_norm_w, w_kv_b, tables)
    att = _per_group(shapes, lambda b, t, s0, prev: _attention(
        q, k, v, b, t, s0, prev, kv_heads=C_HEADS, groups=1, dq=C_QK_PAD, dv=C_V, row_parts=1),
        jnp.zeros((x.shape[0], C_HEADS * C_V), BF16))
    return _matmul_residual_ln(att, w_out.astype(BF16), x, ln_g, ln_b)


def _trunks(xs, params):
    (w_in_even, lb_logits, hgrn_norm_w, q_norm_w, k_norm_w, w_out_even,
     w_in_odd, q_a_norm_w, w_q_b, kv_a_norm_w, w_kv_b, w_out_odd,
     w_router, router_bias, w_gate, w_up, w_down, ws_gate, ws_up, ws_down,
     ln_mix_g, ln_mix_b, ln_ffn_g, ln_ffn_b) = params
    shapes = [(x.shape[0], x.shape[1]) for x in xs]
    for (b, t) in shapes:
        assert t % TOKEN_TILE == 0 and t % GRID_W == 0
    d = xs[0].shape[-1]
    x = jnp.concatenate([a.reshape(-1, d) for a in xs], axis=0)
    xb = x.astype(BF16)
    t_max = max(t for _, t in shapes)
    tables_b = _rope_cs(t_max, B_HEAD_DIM)
    tables_c = _rope_cs(t_max, C_ROPE)
    lb_all = jnp.cumsum(jax.nn.softmax(lb_logits.astype(F32), axis=0), axis=0)
    for l in range(DEPTH):
        j = l // 2
        if l % 2 == 0:
            x, xb = _even_mixer(x, xb, shapes, tables_b, w_in_even[j], lb_all[j], hgrn_norm_w[j],
                                q_norm_w[j], k_norm_w[j], w_out_even[j], ln_mix_g[l], ln_mix_b[l])
        else:
            x, xb = _mla_mixer(x, xb, shapes, tables_c, w_in_odd[j], q_a_norm_w[j], w_q_b[j],
                               kv_a_norm_w[j], w_kv_b[j], w_out_odd[j], ln_mix_g[l], ln_mix_b[l])
        x, xb = _moe_block(x, xb, l, w_router[l], router_bias[l], w_gate, w_up, w_down,
                           ws_gate[l], ws_up[l], ws_down[l], ln_ffn_g[l], ln_ffn_b[l])
    outs = []
    off = 0
    for (b, t) in shapes:
        outs.append(x[off:off + b * t].reshape(b, t, d))
        off += b * t
    return tuple(outs)


def kernel(x_prompt, x_sample, w_in_even, lb_logits, hgrn_norm_w, q_norm_w, k_norm_w, w_out_even,
           w_in_odd, q_a_norm_w, w_q_b, kv_a_norm_w, w_kv_b, w_out_odd,
           w_router, router_bias, w_gate, w_up, w_down, ws_gate, ws_up, ws_down,
           ln_mix_g, ln_mix_b, ln_ffn_g, ln_ffn_b):
    params = (w_in_even, lb_logits, hgrn_norm_w, q_norm_w, k_norm_w, w_out_even,
              w_in_odd, q_a_norm_w, w_q_b, kv_a_norm_w, w_kv_b, w_out_odd,
              w_router, router_bias, w_gate, w_up, w_down, ws_gate, ws_up, ws_down,
              ln_mix_g, ln_mix_b, ln_ffn_g, ln_ffn_b)
    return _trunks([x_prompt, x_sample], params)
```

```python
import functools

import jax
import jax.numpy as jnp
from jax import lax
from jax.experimental import pallas as pl
from jax.experimental.pallas import tpu as pltpu

F32 = jnp.float32
BF16 = jnp.bfloat16

GRID_W = 64
ROPE_THETA = 10000.0
RMS_EPS = 1e-6
LN_EPS = 1e-5
DEPTH = 2
ALPHA = (2 * DEPTH) ** 0.25

A_HEADS = 4
A_DIM = 128
A_WIDTH = A_HEADS * A_DIM
A_COLS = 5 * A_WIDTH
B_HEADS = 4
B_KV_HEADS = 2
B_GROUP = B_HEADS // B_KV_HEADS
B_HEAD_DIM = 128
B_WIDTH = B_HEADS * B_HEAD_DIM
B_KVW = B_KV_HEADS * B_HEAD_DIM
C_HEADS = 8
C_NOPE = 128
C_ROPE = 64
C_V = 128
C_Q_LORA = 384
C_KV_LORA = 256
C_QK_PAD = 256
N_EXPERTS = 256
TOP_K = 8
N_GROUPS = 8
TOPK_GROUPS = 4
ROUTED_SCALE = 2.5

V7X_VMEM_LIMIT_BYTES = 56 * 1024 * 1024
LANES = 128
SUBLANES = 8

TOKEN_TILE = 512
GLA_CHUNK = 64
GLA_SUB = 16
MOE_ROWS = 512
MOE_PARTS = 4
NEG_BIG = -1e30
LOG2_E = 1.4426950408889634

_NT = (((1,), (1,)), ((), ()))


def _params(*semantics):
    return pltpu.CompilerParams(dimension_semantics=semantics,
                                vmem_limit_bytes=V7X_VMEM_LIMIT_BYTES)


def _tile(n, want):
    t = min(n, want)
    assert n % t == 0, (n, t)
    return t


def _rows(tm, width):
    return pl.BlockSpec((tm, width), lambda i: (i, 0))


def _const(shape):
    return pl.BlockSpec(shape, lambda i: (0,) * len(shape))


def _layer_norm(x, g, b):
    mu = jnp.mean(x, axis=-1, keepdims=True)
    xc = x - mu
    var = jnp.mean(xc * xc, axis=-1, keepdims=True)
    return xc * lax.rsqrt(var + LN_EPS) * g + b


def _inv_rms(x):
    return lax.rsqrt(jnp.mean(x * x, axis=-1, keepdims=True) + RMS_EPS)


def _rope_swap(n):
    q = n // 4
    a = jnp.arange(q)
    return jnp.concatenate([a + q, a, a + 3 * q, a + 2 * q])


def _rope_cs(t, rot_dim):
    row = (jnp.arange(t) // GRID_W).astype(F32)
    col = (jnp.arange(t) % GRID_W).astype(F32)
    half = rot_dim // 2
    inv = ROPE_THETA ** (-jnp.arange(0, half, 2, dtype=F32) / half)
    ar = row[:, None] * inv[None, :]
    ac = col[:, None] * inv[None, :]
    c = jnp.concatenate([jnp.cos(ar), jnp.cos(ar), jnp.cos(ac), jnp.cos(ac)], axis=1)
    s = jnp.concatenate([-jnp.sin(ar), jnp.sin(ar), -jnp.sin(ac), jnp.sin(ac)], axis=1)
    pad = ((0, 0), (0, LANES - rot_dim))
    return jnp.pad(c, pad), jnp.pad(s, pad)


def _pos_block(shapes, tm):
    groups = []
    first = 0
    for (b, t) in shapes:
        groups.append((first, t // tm))
        first += b * t // tm

    def index_map(i):
        first_tile, per_seq = groups[0]
        blk = (i - first_tile) % per_seq
        for first_tile, per_seq in groups[1:]:
            blk = jnp.where(i >= first_tile, (i - first_tile) % per_seq, blk)
        return (blk, 0)

    return index_map


def _even_prep_kernel(x_ref, w_ref, qw_ref, kw_ref, c_ref, s_ref, ha_ref, q_ref, k_ref, v_ref):
    xb = x_ref[...]
    for j in range(0, A_COLS, A_WIDTH):
        ha_ref[:, j:j + A_WIDTH] = jnp.dot(xb, w_ref[:, j:j + A_WIDTH], preferred_element_type=F32)
    c = c_ref[...]
    s = s_ref[...]
    q0, k0, v0 = A_COLS, A_COLS + B_WIDTH, A_COLS + B_WIDTH + B_KVW
    qs0 = v0 + B_KVW
    ks0 = qs0 + B_WIDTH
    d = B_HEAD_DIM

    def normed_rope(col, col_sw, heads, w2_ref, scale, o_ref):
        y = jnp.dot(xb, w_ref[:, col:col + heads * d], preferred_element_type=F32)
        ysw = jnp.dot(xb, w_ref[:, col_sw:col_sw + heads * d], preferred_element_type=F32)
        for h in range(heads):
            sl = slice(h * d, (h + 1) * d)
            yh = y[:, sl]
            rot = yh * (w2_ref[0:1] * c) + ysw[:, sl] * (w2_ref[1:2] * s)
            o_ref[:, sl] = (rot * (_inv_rms(yh) * scale)).astype(BF16)

    normed_rope(q0, qs0, B_HEADS, qw_ref, B_HEAD_DIM ** -0.5 * LOG2_E, q_ref)
    normed_rope(k0, ks0, B_KV_HEADS, kw_ref, 1.0, k_ref)
    v_ref[...] = jnp.dot(xb, w_ref[:, v0:v0 + B_KVW], preferred_element_type=F32).astype(BF16)


def _even_prep(xb, shapes, w_in, q_norm_w, k_norm_w, tables):
    n, dm = xb.shape
    tm = TOKEN_TILE
    swap = _rope_swap(B_HEAD_DIM)
    q0, k0 = A_COLS, A_COLS + B_WIDTH
    wq = w_in[:, q0:q0 + B_WIDTH].reshape(dm, B_HEADS, B_HEAD_DIM)[:, :, swap].reshape(dm, B_WIDTH)
    wk = w_in[:, k0:k0 + B_KVW].reshape(dm, B_KV_HEADS, B_HEAD_DIM)[:, :, swap].reshape(dm, B_KVW)
    w_ext = jnp.concatenate([w_in, wq, wk], axis=1).astype(BF16)
    qw = jnp.stack([q_norm_w, q_norm_w[swap]]).astype(F32)
    kw = jnp.stack([k_norm_w, k_norm_w[swap]]).astype(F32)
    pos = pl.BlockSpec((tm, LANES), _pos_block(shapes, tm))
    return pl.pallas_call(
        _even_prep_kernel,
        grid=(n // tm,),
        in_specs=[_rows(tm, dm), _const(w_ext.shape), _const((2, B_HEAD_DIM)), _const((2, B_HEAD_DIM)),
                  pos, pos],
        out_specs=[_rows(tm, A_COLS), _rows(tm, B_WIDTH), _rows(tm, B_KVW), _rows(tm, B_KVW)],
        out_shape=[jax.ShapeDtypeStruct((n, A_COLS), F32), jax.ShapeDtypeStruct((n, B_WIDTH), BF16),
                   jax.ShapeDtypeStruct((n, B_KVW), BF16), jax.ShapeDtypeStruct((n, B_KVW), BF16)],
        compiler_params=_params("parallel"),
    )(xb, w_ext, qw, kw, tables[0], tables[1])


def _even_out_kernel(of_ref, ob_ref, g_ref, att_ref, nw_ref, w_ref, r_ref, lg_ref, lb_ref, o_ref, ob16_ref):
    acc = ALPHA * r_ref[...] + jnp.dot(att_ref[...], w_ref[A_WIDTH:, :], preferred_element_type=F32)
    nw = nw_ref[...]
    parts = []
    for h in range(A_HEADS):
        sl = slice(h * A_DIM, (h + 1) * A_DIM)
        o = of_ref[:, sl] + ob_ref[:, sl]
        g = g_ref[:, sl]
        parts.append((o * _inv_rms(o) * nw * (g * jax.nn.sigmoid(g))).astype(BF16))
    acc += jnp.dot(jnp.concatenate(parts, axis=1), w_ref[:A_WIDTH, :], preferred_element_type=F32)
    out = _layer_norm(acc, lg_ref[...], lb_ref[...])
    o_ref[...] = out
    ob16_ref[...] = out.astype(BF16)


def _even_out(o_f, o_b, ha, att, hgrn_norm_w, w_out, x, ln_g, ln_b):
    n, d = x.shape
    tm = TOKEN_TILE
    gate = pl.BlockSpec((tm, A_WIDTH), lambda i: (i, 4))
    return pl.pallas_call(
        _even_out_kernel,
        grid=(n // tm,),
        in_specs=[_rows(tm, A_WIDTH), _rows(tm, A_WIDTH), gate, _rows(tm, B_WIDTH), _const((1, A_DIM)),
                  _const(w_out.shape), _rows(tm, d), _const((1, d)), _const((1, d))],
        out_specs=[_rows(tm, d), _rows(tm, d)],
        out_shape=[jax.ShapeDtypeStruct((n, d), F32), jax.ShapeDtypeStruct((n, d), BF16)],
        compiler_params=_params("parallel"),
    )(o_f, o_b, ha, att, hgrn_norm_w.reshape(1, A_DIM).astype(F32), w_out.astype(BF16), x,
      ln_g.reshape(1, d), ln_b.reshape(1, d))


def _mla_prep_kernel(x_ref, win_ref, qn_ref, kn_ref, wq_ref, wkv_ref, c_ref, s_ref, q_ref, k_ref, v_ref):
    xb = x_ref[...]
    c = c_ref[...]
    s = s_ref[...]
    hc = jnp.dot(xb, win_ref[...], preferred_element_type=F32)
    cq = hc[:, :C_Q_LORA]
    cq = (cq * _inv_rms(cq) * qn_ref[...]).astype(BF16)
    ckv = hc[:, C_Q_LORA:C_Q_LORA + C_KV_LORA]
    ckv = (ckv * _inv_rms(ckv) * kn_ref[...]).astype(BF16)
    kr0 = C_Q_LORA + C_KV_LORA
    k_rope = (hc[:, kr0:kr0 + LANES] * c + hc[:, kr0 + LANES:kr0 + 2 * LANES] * s).astype(BF16)
    scale = (C_NOPE + C_ROPE) ** -0.5 * LOG2_E
    hw = C_HEADS * C_NOPE
    q_nope = jnp.dot(cq, wq_ref[:, :hw], preferred_element_type=F32)
    q_r = jnp.dot(cq, wq_ref[:, hw:2 * hw], preferred_element_type=F32)
    q_rs = jnp.dot(cq, wq_ref[:, 2 * hw:], preferred_element_type=F32)
    k_nope = jnp.dot(ckv, wkv_ref[:, :hw], preferred_element_type=F32)
    v_ref[...] = jnp.dot(ckv, wkv_ref[:, hw:], preferred_element_type=F32).astype(BF16)
    for h in range(C_HEADS):
        sl = slice(h * LANES, (h + 1) * LANES)
        lo = h * C_QK_PAD
        q_ref[:, lo:lo + LANES] = (q_nope[:, sl] * scale).astype(BF16)
        q_ref[:, lo + LANES:lo + 2 * LANES] = ((q_r[:, sl] * c + q_rs[:, sl] * s) * scale).astype(BF16)
        k_ref[:, lo:lo + LANES] = k_nope[:, sl].astype(BF16)
        k_ref[:, lo + LANES:lo + 2 * LANES] = k_rope


def _mla_prep(xb, shapes, w_in, q_a_norm_w, w_q_b, kv_a_norm_w, w_kv_b, tables):
    n, dm = xb.shape
    tm = TOKEN_TILE
    swap = _rope_swap(C_ROPE)
    lane_pad = lambda w: jnp.pad(w, ((0, 0),) * (w.ndim - 1) + ((0, LANES - C_ROPE),))
    kr0 = C_Q_LORA + C_KV_LORA
    w_kr = w_in[:, kr0:]
    win = jnp.concatenate([w_in[:, :kr0], lane_pad(w_kr), lane_pad(w_kr[:, swap])], axis=1).astype(BF16)
    wq = w_q_b.reshape(C_Q_LORA, C_HEADS, C_NOPE + C_ROPE)
    wq_r = wq[:, :, C_NOPE:]
    hw = C_HEADS * LANES
    wq_all = jnp.concatenate([wq[:, :, :C_NOPE].reshape(C_Q_LORA, hw),
                              lane_pad(wq_r).reshape(C_Q_LORA, hw),
                              lane_pad(wq_r[:, :, swap]).reshape(C_Q_LORA, hw)], axis=1).astype(BF16)
    wkv = w_kv_b.reshape(C_KV_LORA, C_HEADS, C_NOPE + C_V)
    wkv_all = jnp.concatenate([wkv[:, :, :C_NOPE].reshape(C_KV_LORA, hw),
                               wkv[:, :, C_NOPE:].reshape(C_KV_LORA, hw)], axis=1).astype(BF16)
    pos = pl.BlockSpec((tm, LANES), _pos_block(shapes, tm))
    qk = jax.ShapeDtypeStruct((n, C_HEADS * C_QK_PAD), BF16)
    return pl.pallas_call(
        _mla_prep_kernel,
        grid=(n // tm,),
        in_specs=[_rows(tm, dm), _const(win.shape), _const((1, C_Q_LORA)), _const((1, C_KV_LORA)),
                  _const(wq_all.shape), _const(wkv_all.shape), pos, pos],
        out_specs=[_rows(tm, C_HEADS * C_QK_PAD), _rows(tm, C_HEADS * C_QK_PAD), _rows(tm, C_HEADS * C_V)],
        out_shape=[qk, qk, jax.ShapeDtypeStruct((n, C_HEADS * C_V), BF16)],
        compiler_params=_params("parallel"),
    )(xb, win, q_a_norm_w.reshape(1, -1).astype(F32), kv_a_norm_w.reshape(1, -1).astype(F32),
      wq_all, wkv_all, tables[0], tables[1])


def _mm_ln_kernel(x_ref, w_ref, r_ref, g_ref, b_ref, o_ref, ob_ref):
    y = ALPHA * r_ref[...] + jnp.dot(x_ref[...], w_ref[...], preferred_element_type=F32)
    out = _layer_norm(y, g_ref[...], b_ref[...])
    o_ref[...] = out
    ob_ref[...] = out.astype(BF16)


def _matmul_residual_ln(x, w, resid, g, b):
    m, k = x.shape
    d = w.shape[1]
    tm = TOKEN_TILE
    return pl.pallas_call(
        _mm_ln_kernel,
        grid=(m // tm,),
        in_specs=[_rows(tm, k), _const((k, d)), _rows(tm, d), _const((1, d)), _const((1, d))],
        out_specs=[_rows(tm, d), _rows(tm, d)],
        out_shape=[jax.ShapeDtypeStruct((m, d), F32), jax.ShapeDtypeStruct((m, d), BF16)],
        compiler_params=_params("parallel"),
    )(x, w, resid, g.reshape(1, d), b.reshape(1, d))


def _row_iota(rows, cols):
    return lax.broadcasted_iota(jnp.int32, (rows, cols), 0).astype(F32)


def _first_max(cur, iota, n):
    m = jnp.max(cur, axis=0, keepdims=True)
    idx = jnp.min(jnp.where(cur == m, iota, float(n)), axis=0, keepdims=True)
    return m, idx


def _route_kernel(x_ref, wh_ref, wl_ref, bias_ref, e_ref, w_ref, cnt_ref):
    x = x_ref[...]
    tm = x.shape[0]
    xh = x.astype(BF16)
    xl = (x - xh.astype(F32)).astype(BF16)
    wh = wh_ref[...]
    logits = lax.dot_general(wh, xh, _NT, preferred_element_type=F32)
    logits += lax.dot_general(wh, xl, _NT, preferred_element_type=F32)
    logits += lax.dot_general(wl_ref[...], xh, _NT, preferred_element_type=F32)
    scores = jax.nn.sigmoid(logits)
    choice = scores + bias_ref[...]
    gsz = N_EXPERTS // N_GROUPS
    sub_iota = _row_iota(gsz, tm)
    group_rows = []
    for g in range(N_GROUPS):
        grp = choice[g * gsz:(g + 1) * gsz]
        m1, first = _first_max(grp, sub_iota, gsz)
        m2 = jnp.max(jnp.where(sub_iota == first, -jnp.inf, grp), axis=0, keepdims=True)
        group_rows.append(m1 + m2)
    cur = jnp.concatenate(group_rows, axis=0)
    g_iota = _row_iota(N_GROUPS, tm)
    sel = jnp.zeros((N_GROUPS, tm), F32)
    for _ in range(TOPK_GROUPS):
        _, gi = _first_max(cur, g_iota, N_GROUPS)
        pick = g_iota == gi
        sel = jnp.where(pick, 1.0, sel)
        cur = jnp.where(pick, -jnp.inf, cur)
    masked = jnp.concatenate(
        [jnp.where(sel[g:g + 1] > 0.5, choice[g * gsz:(g + 1) * gsz], -jnp.inf)
         for g in range(N_GROUPS)], axis=0)
    e_iota = _row_iota(N_EXPERTS, tm)
    cur = masked
    ids, vals = [], []
    for _ in range(TOP_K):
        _, ei = _first_max(cur, e_iota, N_EXPERTS)
        pick = e_iota == ei
        vals.append(jnp.sum(jnp.where(pick, scores, 0.0), axis=0, keepdims=True))
        ids.append(ei)
        cur = jnp.where(pick, -jnp.inf, cur)
    member = jnp.where((masked > -jnp.inf) & (cur == -jnp.inf), 1.0, 0.0).astype(BF16)
    s = jnp.concatenate(vals, axis=0)
    e_ref[...] = jnp.concatenate(ids, axis=0).astype(jnp.int32)
    w_ref[...] = s / (jnp.sum(s, axis=0, keepdims=True) + 1e-20) * ROUTED_SCALE
    cnt_ref[...] = lax.dot_general(jnp.ones((8, tm), BF16), member, _NT, preferred_element_type=F32)


def _route(x, w_router, router_bias, tm):
    n, d = x.shape
    wt = w_router.T
    wh = wt.astype(BF16)
    wl = (wt - wh.astype(F32)).astype(BF16)
    e, w, cnt = pl.pallas_call(
        _route_kernel,
        grid=(n // tm,),
        in_specs=[_rows(tm, d), _const((N_EXPERTS, d)), _const((N_EXPERTS, d)), _const((N_EXPERTS, 1))],
        out_specs=[pl.BlockSpec((TOP_K, tm), lambda i: (0, i)),
                   pl.BlockSpec((TOP_K, tm), lambda i: (0, i)),
                   pl.BlockSpec((None, 8, N_EXPERTS), lambda i: (i, 0, 0))],
        out_shape=[jax.ShapeDtypeStruct((TOP_K, n), jnp.int32),
                   jax.ShapeDtypeStruct((TOP_K, n), F32),
                   jax.ShapeDtypeStruct((n // tm, 8, N_EXPERTS), F32)],
        compiler_params=_params("parallel"),
    )(x, wh, wl, router_bias.reshape(N_EXPERTS, 1).astype(F32))
    return e, w, cnt[:, 0, :]


def _dest_kernel(e_ref, base_ref, d_ref):
    e = e_ref[...].astype(F32)
    tm = e.shape[1]
    e_iota = _row_iota(N_EXPERTS, tm)
    picks = [e_iota == e[k:k + 1] for k in range(TOP_K)]
    member = jnp.zeros((N_EXPERTS, tm), F32)
    for p in picks:
        member = jnp.where(p, 1.0, member)
    r = lax.broadcasted_iota(jnp.int32, (tm, tm), 0)
    c = lax.broadcasted_iota(jnp.int32, (tm, tm), 1)
    upper = jnp.where(r <= c, 1.0, 0.0).astype(BF16)
    prefix = jnp.dot(member.astype(BF16), upper, preferred_element_type=F32)
    val = prefix + (base_ref[...] - 1.0)
    rows = [jnp.sum(jnp.where(p, val, 0.0), axis=0, keepdims=True) for p in picks]
    d_ref[...] = jnp.concatenate(rows, axis=0).astype(jnp.int32)


def _dest_rows(top_e, tile_base, tm):
    k, n = top_e.shape
    return pl.pallas_call(
        _dest_kernel,
        grid=(n // tm,),
        in_specs=[pl.BlockSpec((k, tm), lambda i: (0, i)),
                  pl.BlockSpec((None, N_EXPERTS, 1), lambda i: (i, 0, 0))],
        out_specs=pl.BlockSpec((k, tm), lambda i: (0, i)),
        out_shape=jax.ShapeDtypeStruct((k, n), jnp.int32),
        compiler_params=_params("parallel"),
    )(top_e, tile_base.astype(F32)[:, :, None])


def _flash_kernel(q_ref, k_ref, v_ref, *rest, groups, dq, dv, tk, row_parts):
    o_ref = rest[-1]
    tq = q_ref.shape[0]
    t = k_ref.shape[0]
    rq = tq // row_parts
    chains = [(g, r * rq) for g in range(groups) for r in range(row_parts)]
    qs = [q_ref[r0:r0 + rq, g * dq:(g + 1) * dq] for g, r0 in chains]

    def body(c, carry):
        start = pl.multiple_of(c * tk, tk)
        k = k_ref[pl.ds(start, tk), :]
        v = v_ref[pl.ds(start, tk), :]
        out = []
        for n in range(len(chains)):
            m, l, acc = carry[n]
            s = lax.dot_general(qs[n], k, _NT, preferred_element_type=F32)
            m_new = jnp.maximum(m, jnp.max(s, axis=-1, keepdims=True))
            alpha = jnp.exp2(m - m_new)
            p = jnp.exp2(s - m_new)
            l = alpha * l + jnp.sum(p, axis=-1, keepdims=True)
            acc = alpha * acc + jnp.dot(p.astype(BF16), v, preferred_element_type=F32)
            out.append((m_new, l, acc))
        return tuple(out)

    init = tuple((jnp.full((rq, 1), NEG_BIG, F32), jnp.zeros((rq, 1), F32),
                  jnp.zeros((rq, dv), F32)) for _ in chains)
    res = lax.fori_loop(0, t // tk, body, init, unroll=2)
    for n, (g, r0) in enumerate(chains):
        _, l, acc = res[n]
        o_ref[r0:r0 + rq, g * dv:(g + 1) * dv] = (acc / l).astype(o_ref.dtype)


def _seq_view(a, t):
    return a.reshape(a.shape[0] // t, t, a.shape[1])


def _carry_output(prev, n_in):
    if prev is None:
        return [], [], {}
    prev = list(prev)
    return prev, [pl.BlockSpec(memory_space=pl.ANY)] * len(prev), {n_in + k: k for k in range(len(prev))}


def _attention(q, k, v, seqs, t, first_seq, prev, *, kv_heads, groups, dq, dv, row_parts, tq=1024):
    n = q.shape[0]
    tq = _tile(t, tq)
    tk = _tile(t, 2048 if t >= 4096 else 1024)
    kern = functools.partial(_flash_kernel, groups=groups, dq=dq, dv=dv, tk=tk, row_parts=row_parts)
    extra, extra_specs, aliases = _carry_output(None if prev is None else [_seq_view(prev, t)], 3)
    out = pl.pallas_call(
        kern,
        grid=(seqs, kv_heads, t // tq),
        in_specs=[pl.BlockSpec((None, tq, groups * dq), lambda i, h, j: (i + first_seq, j, h)),
                  pl.BlockSpec((None, t, dq), lambda i, h, j: (i + first_seq, 0, h)),
                  pl.BlockSpec((None, t, dv), lambda i, h, j: (i + first_seq, 0, h))] + extra_specs,
        out_specs=pl.BlockSpec((None, tq, groups * dv), lambda i, h, j: (i + first_seq, j, h)),
        out_shape=jax.ShapeDtypeStruct((n // t, t, kv_heads * groups * dv), BF16),
        input_output_aliases=aliases,
        compiler_params=_params("parallel", "parallel", "arbitrary"),
    )(_seq_view(q, t), _seq_view(k, t), _seq_view(v, t), *extra)
    return out.reshape(n, -1)


def _cumsum_time(g, rev):
    c = g.shape[0]
    row = lax.broadcasted_iota(jnp.int32, (c, c), 0)
    col = lax.broadcasted_iota(jnp.int32, (c, c), 1)
    tri = jnp.where((col >= row) if rev else (col <= row), 1.0, 0.0).astype(BF16)
    g1 = g.astype(BF16)
    r1 = g - g1.astype(F32)
    g2 = r1.astype(BF16)
    g3 = (r1 - g2.astype(F32)).astype(BF16)
    out = jnp.dot(tri, g1, preferred_element_type=F32)
    out += jnp.dot(tri, g2, preferred_element_type=F32)
    out += jnp.dot(tri, g3, preferred_element_type=F32)
    return out


def _gla_gates(fl, lb, rev):
    fg = lb + (1.0 - lb) * jax.nn.sigmoid(fl)
    return 1.0 - fg, _cumsum_time(jnp.log2(fg), rev)


def _gla_state(q, v, kk, b, st, rev):
    b_edge = b[0:1] if rev else b[GLA_CHUNK - 1:GLA_CHUNK]
    o = lax.dot_general((q * jnp.exp2(b)).astype(BF16), st.astype(BF16), _NT,
                        preferred_element_type=F32)
    kdec = (kk * jnp.exp2(b_edge - b)).astype(BF16)
    st_new = st * jnp.exp2(b_edge) + lax.dot_general(
        v.astype(BF16), kdec, (((0,), (0,)), ((), ())), preferred_element_type=F32)
    return o, st_new


def _gla_earlier_blocks(i, q, v, kk, b, rev):
    cn, sub = GLA_CHUNK, GLA_SUB
    lo, hi = i * sub, (i + 1) * sub
    if rev:
        if hi == cn:
            return None
        ref_b, plo, phi = b[hi:hi + 1], hi, cn
    else:
        if lo == 0:
            return None
        ref_b, plo, phi = b[lo - 1:lo], 0, lo
    qd = (q[lo:hi] * jnp.exp2(b[lo:hi] - ref_b)).astype(BF16)
    kd = (kk[plo:phi] * jnp.exp2(ref_b - b[plo:phi])).astype(BF16)
    a = lax.dot_general(qd, kd, _NT, preferred_element_type=F32)
    return jnp.dot(a.astype(BF16), v[plo:phi].astype(BF16), preferred_element_type=F32)


def _gla_diagonal(i, q, v, kk, b, ones, rev):
    sub = GLA_SUB
    lo = i * sub
    bi, qi, ki, vi = b[lo:lo + sub], q[lo:lo + sub], kk[lo:lo + sub], v[lo:lo + sub]
    t_idx = lax.broadcasted_iota(jnp.int32, (SUBLANES, 1), 0)
    terms, plan = [], []
    for s in range(sub):
        for r0 in range(0, sub, SUBLANES):
            if (r0 > s) if rev else (r0 + SUBLANES - 1 < s):
                continue
            d = bi[r0:r0 + SUBLANES] - bi[s:s + 1]
            if r0 <= s < r0 + SUBLANES:
                keep = (t_idx + r0 <= s) if rev else (t_idx + r0 >= s)
                d = jnp.where(keep, d, NEG_BIG)
            terms.append(qi[r0:r0 + SUBLANES] * jnp.exp2(d) * ki[s:s + 1])
            plan.append((s, r0))
    a = jnp.dot(jnp.concatenate(terms, axis=0).astype(BF16), ones, preferred_element_type=F32)
    halves = {}
    for n, (s, r0) in enumerate(plan):
        term = a[n * SUBLANES:(n + 1) * SUBLANES] * vi[s:s + 1]
        halves[r0] = halves[r0] + term if r0 in halves else term
    return halves


def _hgrn_kernel(qf_ref, ff_ref, vf_ref, qb_ref, fb_ref, vb_ref, lb_ref, *rest):
    of_ref, ob_ref, st_ref = rest[-3:]

    @pl.when(pl.program_id(1) == 0)
    def _():
        st_ref[...] = jnp.zeros_like(st_ref)

    ones = jnp.ones((A_DIM, LANES), BF16)
    dirs = ((qf_ref, ff_ref, vf_ref, of_ref, False), (qb_ref, fb_ref, vb_ref, ob_ref, True))
    work = []
    for di, (q_ref, f_ref, v_ref, o_ref, rev) in enumerate(dirs):
        for h in range(A_HEADS):
            sl = slice(h * A_DIM, (h + 1) * A_DIM)
            kk, b = _gla_gates(f_ref[:, sl], lb_ref[di:di + 1, sl], rev)
            work.append(dict(di=di, h=h, sl=sl, rev=rev, o_ref=o_ref, q=q_ref[:, sl], v=v_ref[:, sl],
                             kk=kk, b=b))
    for w in work:
        w["o"], st_new = _gla_state(w["q"], w["v"], w["kk"], w["b"], st_ref[w["di"], w["h"]], w["rev"])
        st_ref[w["di"], w["h"]] = st_new
    for i in range(GLA_CHUNK // GLA_SUB):
        lo = i * GLA_SUB
        early = [_gla_earlier_blocks(i, w["q"], w["v"], w["kk"], w["b"], w["rev"]) for w in work]
        diag = [_gla_diagonal(i, w["q"], w["v"], w["kk"], w["b"], ones, w["rev"]) for w in work]
        for w, e, dg in zip(work, early, diag):
            oi = w["o"][lo:lo + GLA_SUB]
            if e is not None:
                oi = oi + e
            for r0, val in dg.items():
                w["o_ref"][lo + r0:lo + r0 + SUBLANES, w["sl"]] = oi[r0:r0 + SUBLANES] + val


def _hgrn_scan(ha, lb, seqs, t, first_seq, prev):
    n = ha.shape[0]
    nc = t // GLA_CHUNK
    blk = (None, GLA_CHUNK, A_WIDTH)
    fwd = lambda col: pl.BlockSpec(blk, lambda i, c: (i + first_seq, c, col))
    bwd = lambda col: pl.BlockSpec(blk, lambda i, c: (i + first_seq, nc - 1 - c, col))
    out = jax.ShapeDtypeStruct((n // t, t, A_WIDTH), F32)
    hv = _seq_view(ha, t)
    extra, extra_specs, aliases = _carry_output(
        None if prev is None else [_seq_view(p, t) for p in prev], 7)
    o_f, o_b = pl.pallas_call(
        _hgrn_kernel,
        grid=(seqs, nc),
        in_specs=[fwd(0), fwd(1), fwd(3), bwd(0), bwd(2), bwd(3),
                  pl.BlockSpec((2, A_WIDTH), lambda i, c: (0, 0))] + extra_specs,
        out_specs=[fwd(0), bwd(0)],
        out_shape=[out, out],
        scratch_shapes=[pltpu.VMEM((2, A_HEADS, A_DIM, A_DIM), F32)],
        input_output_aliases=aliases,
        compiler_params=_params("parallel", "arbitrary"),
    )(hv, hv, hv, hv, hv, hv, lb, *extra)
    return o_f.reshape(n, A_WIDTH), o_b.reshape(n, A_WIDTH)


def _swiglu_body(x, wg, wu, wd):
    g = jnp.dot(x, wg, preferred_element_type=F32)
    u = jnp.dot(x, wu, preferred_element_type=F32)
    hid = (g * jax.nn.sigmoid(g) * u).astype(BF16)
    return jnp.dot(hid, wd, preferred_element_type=F32)


def _combine_kernel(y_ref, w_ref, x_ref, xb_ref, wg_ref, wu_ref, wd_ref, g_ref, b_ref, *rest):
    o_ref, ob_ref = rest[-2:]
    w = w_ref[...]
    f = _swiglu_body(xb_ref[...], wg_ref[...], wu_ref[...], wd_ref[...])
    for k in range(TOP_K):
        f = f + y_ref[k].astype(F32) * w[:, k:k + 1]
    out = _layer_norm(ALPHA * x_ref[...] + f, g_ref[...], b_ref[...])
    o_ref[...] = out
    ob_ref[...] = out.astype(BF16)


def _moe_combine(y_tok, w_tok, x, xb, wg, wu, wd, g, b, first_row, prev, tm=256):
    n, d = x.shape
    n_part = y_tok.shape[1]
    f = wg.shape[1]
    tm = _tile(n_part, tm)
    assert first_row % tm == 0
    first = first_row // tm
    rows = lambda width: pl.BlockSpec((tm, width), lambda i: (i + first, 0))
    extra, extra_specs, aliases = _carry_output(prev, 9)
    return pl.pallas_call(
        _combine_kernel,
        grid=(n_part // tm,),
        in_specs=[pl.BlockSpec((TOP_K, tm, d), lambda i: (0, i, 0)), rows(TOP_K), rows(d), rows(d),
                  _const((d, f)), _const((d, f)), _const((f, d)), _const((1, d)), _const((1, d))]
        + extra_specs,
        out_specs=[rows(d), rows(d)],
        out_shape=[jax.ShapeDtypeStruct((n, d), F32), jax.ShapeDtypeStruct((n, d), BF16)],
        input_output_aliases=aliases,
        compiler_params=_params("parallel"),
    )(y_tok, w_tok, x, xb, wg, wu, wd, g.reshape(1, d), b.reshape(1, d), *extra)


def _grouped_kernel(blk_e_ref, nused_ref, x_ref, wg_ref, wu_ref, wd_ref, *rest, first_blk):
    o_ref, wg16, wu16, wd16 = rest[-4:]
    i = pl.program_id(0)
    blk = i + first_blk
    active = blk < nused_ref[0]
    new_expert = (i == 0) | (blk_e_ref[blk] != blk_e_ref[jnp.maximum(blk - 1, 0)])

    @pl.when(active & new_expert)
    def _():
        wg16[...] = wg_ref[0].astype(BF16)
        wu16[...] = wu_ref[0].astype(BF16)
        wd16[...] = wd_ref[0].astype(BF16)

    @pl.when(active)
    def _():
        o_ref[...] = _swiglu_body(x_ref[...], wg16[...], wu16[...], wd16[...]).astype(o_ref.dtype)


def _grouped_experts(x_part, blk_e, nused, layer, wg, wu, wd, first_blk, total_rows, prev):
    p, d = x_part.shape
    f = wg.shape[3]
    nb = p // MOE_ROWS
    active = lambda i, nu: jnp.minimum(i, jnp.clip(nu[0] - first_blk - 1, 0, nb - 1))
    row_in = lambda i, be, nu: (active(i, nu), 0)
    row_out = lambda i, be, nu: (active(i, nu) + first_blk, 0)
    exp = lambda i, be, nu: (layer, be[active(i, nu) + first_blk], 0, 0)
    extra, extra_specs, aliases = _carry_output(None if prev is None else [prev], 6)
    return pl.pallas_call(
        functools.partial(_grouped_kernel, first_blk=first_blk),
        grid_spec=pltpu.PrefetchScalarGridSpec(
            num_scalar_prefetch=2,
            grid=(nb,),
            in_specs=[pl.BlockSpec((MOE_ROWS, d), row_in),
                      pl.BlockSpec((None, 1, d, f), exp),
                      pl.BlockSpec((None, 1, d, f), exp),
                      pl.BlockSpec((None, 1, f, d), exp)] + extra_specs,
            out_specs=pl.BlockSpec((MOE_ROWS, d), row_out),
            scratch_shapes=[pltpu.VMEM((d, f), BF16), pltpu.VMEM((d, f), BF16), pltpu.VMEM((f, d), BF16)]),
        out_shape=jax.ShapeDtypeStruct((total_rows, d), BF16),
        input_output_aliases=aliases,
        compiler_params=_params("arbitrary"),
    )(blk_e, nused, x_part, wg, wu, wd, *extra)


def _moe_block(x, xb, layer, w_router, router_bias, w_gate, w_up, w_down, ws_gate, ws_up, ws_down,
               ln_g, ln_b):
    n, d = x.shape
    tm = TOKEN_TILE
    top_e, top_w, tile_cnt = _route(x, w_router, router_bias, tm)
    tile_cnt = tile_cnt.astype(jnp.int32)
    counts = jnp.sum(tile_cnt, axis=0)
    padded = (counts + MOE_ROWS - 1) // MOE_ROWS * MOE_ROWS
    pend = jnp.cumsum(padded)
    pstart = pend - padded
    tile_base = pstart[None, :] + jnp.cumsum(tile_cnt, axis=0) - tile_cnt
    dest = _dest_rows(top_e, tile_base, tm)
    nblk = (n * TOP_K) // MOE_ROWS + N_EXPERTS
    blk_e = jnp.minimum(jnp.searchsorted(pend, jnp.arange(nblk) * MOE_ROWS, side='right'),
                        N_EXPERTS - 1).astype(jnp.int32)
    nused = (pend[-1] // MOE_ROWS).astype(jnp.int32).reshape(1)
    tok = jnp.broadcast_to(jnp.arange(n, dtype=jnp.int32)[None, :], (TOP_K, n))
    total_rows = nblk * MOE_ROWS
    pad_tok = jnp.arange(total_rows, dtype=jnp.int32) % n
    filled = jnp.zeros((total_rows,), jnp.int32).at[dest.reshape(-1)].add(
        tok.reshape(-1) + 1, unique_indices=True, mode="promise_in_bounds")
    slot_tok = jnp.where(filled > 0, filled - 1, pad_tok)
    assert nblk % MOE_PARTS == 0 and n % MOE_PARTS == 0
    wg, wu, wd = w_gate, w_up, w_down
    part_blks = nblk // MOE_PARTS
    part_rows = part_blks * MOE_ROWS
    y_sorted = None
    for j in range(MOE_PARTS):
        x_part = xb.at[slot_tok[j * part_rows:(j + 1) * part_rows]].get(mode="promise_in_bounds")
        y_sorted = _grouped_experts(x_part, blk_e, nused, layer, wg, wu, wd, j * part_blks, total_rows,
                                    y_sorted)
    w_tok = top_w.T
    sg, su, sd = ws_gate.astype(BF16), ws_up.astype(BF16), ws_down.astype(BF16)
    part_tok = n // MOE_PARTS
    out = None
    for j in range(MOE_PARTS):
        rows = dest[:, j * part_tok:(j + 1) * part_tok].reshape(-1)
        y_tok = y_sorted.at[rows].get(mode="promise_in_bounds").reshape(TOP_K, part_tok, d)
        out = _moe_combine(y_tok, w_tok, x, xb, sg, su, sd, ln_g, ln_b, j * part_tok, out)
    return out


def _per_group(shapes, fn, init):
    out = init
    tok = 0
    for (b, t) in shapes:
        assert tok % t == 0
        out = fn(b, t, tok // t, out)
        tok += b * t
    return out


def _even_mixer(x, xb, shapes, tables, w_in, lb, hgrn_norm_w, q_norm_w, k_norm_w, w_out, ln_g, ln_b):
    ha, q, k, v = _even_prep(xb, shapes, w_in, q_norm_w, k_norm_w, tables)
    lb2 = lb.reshape(2, A_WIDTH)
    n = x.shape[0]
    o_f, o_b = _per_group(shapes, lambda b, t, s0, prev: _hgrn_scan(ha, lb2, b, t, s0, prev),
                          (jnp.zeros((n, A_WIDTH), F32), jnp.zeros((n, A_WIDTH), F32)))
    att = _per_group(shapes, lambda b, t, s0, prev: _attention(
        q, k, v, b, t, s0, prev, kv_heads=B_KV_HEADS, groups=B_GROUP, dq=B_HEAD_DIM, dv=B_HEAD_DIM,
        row_parts=1), jnp.zeros((n, B_WIDTH), BF16))
    return _even_out(o_f, o_b, ha, att, hgrn_norm_w, w_out, x, ln_g, ln_b)


def _mla_mixer(x, xb, shapes, tables, w_in, q_a_norm_w, w_q_b, kv_a_norm_w, w_kv_b, w_out, ln_g, ln_b):
    q, k, v = _mla_prep(xb, shapes, w_in, q_a_norm_w, w_q_b, kv_a_norm_w, w_kv_b, tables)
    att = _per_group(shapes, lambda b, t, s0, prev: _attention(
        q, k, v, b, t, s0, prev, kv_heads=C_HEADS, groups=1, dq=C_QK_PAD, dv=C_V, row_parts=1),
        jnp.zeros((x.shape[0], C_HEADS * C_V), BF16))
    return _matmul_residual_ln(att, w_out.astype(BF16), x, ln_g, ln_b)


def _trunks(xs, params):
    (w_in_even, lb_logits, hgrn_norm_w, q_norm_w, k_norm_w, w_out_even,
     w_in_odd, q_a_norm_w, w_q_b, kv_a_norm_w, w_kv_b, w_out_odd,
     w_router, router_bias, w_gate, w_up, w_down, ws_gate, ws_up, ws_down,
     ln_mix_g, ln_mix_b, ln_ffn_g, ln_ffn_b) = params
    shapes = [(x.shape[0], x.shape[1]) for x in xs]
    for (b, t) in shapes:
        assert t % TOKEN_TILE == 0 and t % GRID_W == 0
    d = xs[0].shape[-1]
    x = jnp.concatenate([a.reshape(-1, d) for a in xs], axis=0)
    xb = x.astype(BF16)
    t_max = max(t for _, t in shapes)
    tables_b = _rope_cs(t_max, B_HEAD_DIM)
    tables_c = _rope_cs(t_max, C_ROPE)
    lb_all = jnp.cumsum(jax.nn.softmax(lb_logits.astype(F32), axis=0), axis=0)
    for l in range(DEPTH):
        j = l // 2
        if l % 2 == 0:
            x, xb = _even_mixer(x, xb, shapes, tables_b, w_in_even[j], lb_all[j], hgrn_norm_w[j],
                                q_norm_w[j], k_norm_w[j], w_out_even[j], ln_mix_g[l], ln_mix_b[l])
        else:
            x, xb = _mla_mixer(x, xb, shapes, tables_c, w_in_odd[j], q_a_norm_w[j], w_q_b[j],
                               kv_a_norm_w[j], w_kv_b[j], w_out_odd[j], ln_mix_g[l], ln_mix_b[l])
        x, xb = _moe_block(x, xb, l, w_router[l], router_bias[l], w_gate, w_up, w_down,
                           ws_gate[l], ws_up[l], ws_down[l], ln_ffn_g[l], ln_ffn_b[l])
    outs = []
    off = 0
    for (b, t) in shapes:
        outs.append(x[off:off + b * t].reshape(b, t, d))
        off += b * t
    return tuple(outs)


def kernel(x_prompt, x_sample, w_in_even, lb_logits, hgrn_norm_w, q_norm_w, k_norm_w, w_out_even,
           w_in_odd, q_a_norm_w, w_q_b, kv_a_norm_w, w_kv_b, w_out_odd,
           w_router, router_bias, w_gate, w_up, w_down, ws_gate, ws_up, ws_down,
           ln_mix_g, ln_mix_b, ln_ffn_g, ln_ffn_b):
    params = (w_in_even, lb_logits, hgrn_norm_w, q_norm_w, k_norm_w, w_out_even,
              w_in_odd, q_a_norm_w, w_q_b, kv_a_norm_w, w_kv_b, w_out_odd,
              w_router, router_bias, w_gate, w_up, w_down, ws_gate, ws_up, ws_down,
              ln_mix_g, ln_mix_b, ln_ffn_g, ln_ffn_b)
    return _trunks([x_prompt, x_sample], params)
```

```python
import functools

import jax
import jax.numpy as jnp
from jax import lax
from jax.experimental import pallas as pl
from jax.experimental.pallas import tpu as pltpu

F32 = jnp.float32
BF16 = jnp.bfloat16

GRID_W = 64
ROPE_THETA = 10000.0
RMS_EPS = 1e-6
LN_EPS = 1e-5
DEPTH = 2
ALPHA = (2 * DEPTH) ** 0.25

A_HEADS = 4
A_DIM = 128
A_WIDTH = A_HEADS * A_DIM
A_COLS = 5 * A_WIDTH
B_HEADS = 4
B_KV_HEADS = 2
B_GROUP = B_HEADS // B_KV_HEADS
B_HEAD_DIM = 128
B_WIDTH = B_HEADS * B_HEAD_DIM
B_KVW = B_KV_HEADS * B_HEAD_DIM
C_HEADS = 8
C_NOPE = 128
C_ROPE = 64
C_V = 128
C_Q_LORA = 384
C_KV_LORA = 256
C_QK_PAD = 256
N_EXPERTS = 256
TOP_K = 8
N_GROUPS = 8
TOPK_GROUPS = 4
ROUTED_SCALE = 2.5

V7X_VMEM_LIMIT_BYTES = 56 * 1024 * 1024
LANES = 128
SUBLANES = 8

TOKEN_TILE = 512
GLA_CHUNK = 64
GLA_SUB = 16
MOE_ROWS = 512
MOE_PARTS = 4
NEG_BIG = -1e30
LOG2_E = 1.4426950408889634

_NT = (((1,), (1,)), ((), ()))


def _params(*semantics):
    return pltpu.CompilerParams(dimension_semantics=semantics,
                                vmem_limit_bytes=V7X_VMEM_LIMIT_BYTES)


def _tile(n, want):
    t = min(n, want)
    assert n % t == 0, (n, t)
    return t


def _rows(tm, width):
    return pl.BlockSpec((tm, width), lambda i: (i, 0))


def _const(shape):
    return pl.BlockSpec(shape, lambda i: (0,) * len(shape))


def _layer_norm(x, g, b):
    mu = jnp.mean(x, axis=-1, keepdims=True)
    xc = x - mu
    var = jnp.mean(xc * xc, axis=-1, keepdims=True)
    return xc * lax.rsqrt(var + LN_EPS) * g + b


def _inv_rms(x):
    return lax.rsqrt(jnp.mean(x * x, axis=-1, keepdims=True) + RMS_EPS)


def _rope_swap(n):
    q = n // 4
    a = jnp.arange(q)
    return jnp.concatenate([a + q, a, a + 3 * q, a + 2 * q])


def _rope_cs(t, rot_dim):
    row = (jnp.arange(t) // GRID_W).astype(F32)
    col = (jnp.arange(t) % GRID_W).astype(F32)
    half = rot_dim // 2
    inv = ROPE_THETA ** (-jnp.arange(0, half, 2, dtype=F32) / half)
    ar = row[:, None] * inv[None, :]
    ac = col[:, None] * inv[None, :]
    c = jnp.concatenate([jnp.cos(ar), jnp.cos(ar), jnp.cos(ac), jnp.cos(ac)], axis=1)
    s = jnp.concatenate([-jnp.sin(ar), jnp.sin(ar), -jnp.sin(ac), jnp.sin(ac)], axis=1)
    pad = ((0, 0), (0, LANES - rot_dim))
    return jnp.pad(c, pad), jnp.pad(s, pad)


def _pos_block(shapes, tm):
    groups = []
    first = 0
    for (b, t) in shapes:
        groups.append((first, t // tm))
        first += b * t // tm

    def index_map(i):
        first_tile, per_seq = groups[0]
        blk = (i - first_tile) % per_seq
        for first_tile, per_seq in groups[1:]:
            blk = jnp.where(i >= first_tile, (i - first_tile) % per_seq, blk)
        return (blk, 0)

    return index_map


def _even_prep_kernel(x_ref, w_ref, qw_ref, kw_ref, c_ref, s_ref, ha_ref, q_ref, k_ref, v_ref):
    xb = x_ref[...]
    for j in range(0, A_COLS, A_WIDTH):
        ha_ref[:, j:j + A_WIDTH] = jnp.dot(xb, w_ref[:, j:j + A_WIDTH], preferred_element_type=F32)
    c = c_ref[...]
    s = s_ref[...]
    q0, k0, v0 = A_COLS, A_COLS + B_WIDTH, A_COLS + B_WIDTH + B_KVW
    qs0 = v0 + B_KVW
    ks0 = qs0 + B_WIDTH
    d = B_HEAD_DIM

    def normed_rope(col, col_sw, heads, w2_ref, scale, o_ref):
        y = jnp.dot(xb, w_ref[:, col:col + heads * d], preferred_element_type=F32)
        ysw = jnp.dot(xb, w_ref[:, col_sw:col_sw + heads * d], preferred_element_type=F32)
        for h in range(heads):
            sl = slice(h * d, (h + 1) * d)
            yh = y[:, sl]
            rot = yh * (w2_ref[0:1] * c) + ysw[:, sl] * (w2_ref[1:2] * s)
            o_ref[:, sl] = (rot * (_inv_rms(yh) * scale)).astype(BF16)

    normed_rope(q0, qs0, B_HEADS, qw_ref, B_HEAD_DIM ** -0.5 * LOG2_E, q_ref)
    normed_rope(k0, ks0, B_KV_HEADS, kw_ref, 1.0, k_ref)
    v_ref[...] = jnp.dot(xb, w_ref[:, v0:v0 + B_KVW], preferred_element_type=F32).astype(BF16)


def _even_prep(xb, shapes, w_in, q_norm_w, k_norm_w, tables):
    n, dm = xb.shape
    tm = TOKEN_TILE
    swap = _rope_swap(B_HEAD_DIM)
    q0, k0 = A_COLS, A_COLS + B_WIDTH
    wq = w_in[:, q0:q0 + B_WIDTH].reshape(dm, B_HEADS, B_HEAD_DIM)[:, :, swap].reshape(dm, B_WIDTH)
    wk = w_in[:, k0:k0 + B_KVW].reshape(dm, B_KV_HEADS, B_HEAD_DIM)[:, :, swap].reshape(dm, B_KVW)
    w_ext = jnp.concatenate([w_in, wq, wk], axis=1).astype(BF16)
    qw = jnp.stack([q_norm_w, q_norm_w[swap]]).astype(F32)
    kw = jnp.stack([k_norm_w, k_norm_w[swap]]).astype(F32)
    pos = pl.BlockSpec((tm, LANES), _pos_block(shapes, tm))
    return pl.pallas_call(
        _even_prep_kernel,
        grid=(n // tm,),
        in_specs=[_rows(tm, dm), _const(w_ext.shape), _const((2, B_HEAD_DIM)), _const((2, B_HEAD_DIM)),
                  pos, pos],
        out_specs=[_rows(tm, A_COLS), _rows(tm, B_WIDTH), _rows(tm, B_KVW), _rows(tm, B_KVW)],
        out_shape=[jax.ShapeDtypeStruct((n, A_COLS), F32), jax.ShapeDtypeStruct((n, B_WIDTH), BF16),
                   jax.ShapeDtypeStruct((n, B_KVW), BF16), jax.ShapeDtypeStruct((n, B_KVW), BF16)],
        compiler_params=_params("parallel"),
    )(xb, w_ext, qw, kw, tables[0], tables[1])


def _even_out_kernel(of_ref, ob_ref, g_ref, att_ref, nw_ref, w_ref, r_ref, lg_ref, lb_ref, o_ref, ob16_ref):
    acc = ALPHA * r_ref[...] + jnp.dot(att_ref[...], w_ref[A_WIDTH:, :], preferred_element_type=F32)
    nw = nw_ref[...]
    parts = []
    for h in range(A_HEADS):
        sl = slice(h * A_DIM, (h + 1) * A_DIM)
        o = of_ref[:, sl] + ob_ref[:, sl]
        g = g_ref[:, sl]
        parts.append((o * _inv_rms(o) * nw * (g * jax.nn.sigmoid(g))).astype(BF16))
    acc += jnp.dot(jnp.concatenate(parts, axis=1), w_ref[:A_WIDTH, :], preferred_element_type=F32)
    out = _layer_norm(acc, lg_ref[...], lb_ref[...])
    o_ref[...] = out
    ob16_ref[...] = out.astype(BF16)


def _even_out(o_f, o_b, ha, att, hgrn_norm_w, w_out, x, ln_g, ln_b):
    n, d = x.shape
    tm = TOKEN_TILE
    gate = pl.BlockSpec((tm, A_WIDTH), lambda i: (i, 4))
    return pl.pallas_call(
        _even_out_kernel,
        grid=(n // tm,),
        in_specs=[_rows(tm, A_WIDTH), _rows(tm, A_WIDTH), gate, _rows(tm, B_WIDTH), _const((1, A_DIM)),
                  _const(w_out.shape), _rows(tm, d), _const((1, d)), _const((1, d))],
        out_specs=[_rows(tm, d), _rows(tm, d)],
        out_shape=[jax.ShapeDtypeStruct((n, d), F32), jax.ShapeDtypeStruct((n, d), BF16)],
        compiler_params=_params("parallel"),
    )(o_f, o_b, ha, att, hgrn_norm_w.reshape(1, A_DIM).astype(F32), w_out.astype(BF16), x,
      ln_g.reshape(1, d), ln_b.reshape(1, d))


def _mla_prep_kernel(x_ref, win_ref, qn_ref, kn_ref, wq_ref, wkv_ref, c_ref, s_ref, q_ref, k_ref, v_ref):
    xb = x_ref[...]
    c = c_ref[...]
    s = s_ref[...]
    hc = jnp.dot(xb, win_ref[...], preferred_element_type=F32)
    cq = hc[:, :C_Q_LORA]
    cq = (cq * _inv_rms(cq) * qn_ref[...]).astype(BF16)
    ckv = hc[:, C_Q_LORA:C_Q_LORA + C_KV_LORA]
    ckv = (ckv * _inv_rms(ckv) * kn_ref[...]).astype(BF16)
    kr0 = C_Q_LORA + C_KV_LORA
    k_rope = (hc[:, kr0:kr0 + LANES] * c + hc[:, kr0 + LANES:kr0 + 2 * LANES] * s).astype(BF16)
    scale = (C_NOPE + C_ROPE) ** -0.5 * LOG2_E
    hw = C_HEADS * C_NOPE
    q_nope = jnp.dot(cq, wq_ref[:, :hw], preferred_element_type=F32)
    q_r = jnp.dot(cq, wq_ref[:, hw:2 * hw], preferred_element_type=F32)
    q_rs = jnp.dot(cq, wq_ref[:, 2 * hw:], preferred_element_type=F32)
    k_nope = jnp.dot(ckv, wkv_ref[:, :hw], preferred_element_type=F32)
    v_ref[...] = jnp.dot(ckv, wkv_ref[:, hw:], preferred_element_type=F32).astype(BF16)
    for h in range(C_HEADS):
        sl = slice(h * LANES, (h + 1) * LANES)
        lo = h * C_QK_PAD
        q_ref[:, lo:lo + LANES] = (q_nope[:, sl] * scale).astype(BF16)
        q_ref[:, lo + LANES:lo + 2 * LANES] = ((q_r[:, sl] * c + q_rs[:, sl] * s) * scale).astype(BF16)
        k_ref[:, lo:lo + LANES] = k_nope[:, sl].astype(BF16)
        k_ref[:, lo + LANES:lo + 2 * LANES] = k_rope


def _mla_prep(xb, shapes, w_in, q_a_norm_w, w_q_b, kv_a_norm_w, w_kv_b, tables):
    n, dm = xb.shape
    tm = TOKEN_TILE
    swap = _rope_swap(C_ROPE)
    lane_pad = lambda w: jnp.pad(w, ((0, 0),) * (w.ndim - 1) + ((0, LANES - C_ROPE),))
    kr0 = C_Q_LORA + C_KV_LORA
    w_kr = w_in[:, kr0:]
    win = jnp.concatenate([w_in[:, :kr0], lane_pad(w_kr), lane_pad(w_kr[:, swap])], axis=1).astype(BF16)
    wq = w_q_b.reshape(C_Q_LORA, C_HEADS, C_NOPE + C_ROPE)
    wq_r = wq[:, :, C_NOPE:]
    hw = C_HEADS * LANES
    wq_all = jnp.concatenate([wq[:, :, :C_NOPE].reshape(C_Q_LORA, hw),
                              lane_pad(wq_r).reshape(C_Q_LORA, hw),
                              lane_pad(wq_r[:, :, swap]).reshape(C_Q_LORA, hw)], axis=1).astype(BF16)
    wkv = w_kv_b.reshape(C_KV_LORA, C_HEADS, C_NOPE + C_V)
    wkv_all = jnp.concatenate([wkv[:, :, :C_NOPE].reshape(C_KV_LORA, hw),
                               wkv[:, :, C_NOPE:].reshape(C_KV_LORA, hw)], axis=1).astype(BF16)
    pos = pl.BlockSpec((tm, LANES), _pos_block(shapes, tm))
    qk = jax.ShapeDtypeStruct((n, C_HEADS * C_QK_PAD), BF16)
    return pl.pallas_call(
        _mla_prep_kernel,
        grid=(n // tm,),
        in_specs=[_rows(tm, dm), _const(win.shape), _const((1, C_Q_LORA)), _const((1, C_KV_LORA)),
                  _const(wq_all.shape), _const(wkv_all.shape), pos, pos],
        out_specs=[_rows(tm, C_HEADS * C_QK_PAD), _rows(tm, C_HEADS * C_QK_PAD), _rows(tm, C_HEADS * C_V)],
        out_shape=[qk, qk, jax.ShapeDtypeStruct((n, C_HEADS * C_V), BF16)],
        compiler_params=_params("parallel"),
    )(xb, win, q_a_norm_w.reshape(1, -1).astype(F32), kv_a_norm_w.reshape(1, -1).astype(F32),
      wq_all, wkv_all, tables[0], tables[1])


def _mm_ln_kernel(x_ref, w_ref, r_ref, g_ref, b_ref, o_ref, ob_ref):
    y = ALPHA * r_ref[...] + jnp.dot(x_ref[...], w_ref[...], preferred_element_type=F32)
    out = _layer_norm(y, g_ref[...], b_ref[...])
    o_ref[...] = out
    ob_ref[...] = out.astype(BF16)


def _matmul_residual_ln(x, w, resid, g, b):
    m, k = x.shape
    d = w.shape[1]
    tm = TOKEN_TILE
    return pl.pallas_call(
        _mm_ln_kernel,
        grid=(m // tm,),
        in_specs=[_rows(tm, k), _const((k, d)), _rows(tm, d), _const((1, d)), _const((1, d))],
        out_specs=[_rows(tm, d), _rows(tm, d)],
        out_shape=[jax.ShapeDtypeStruct((m, d), F32), jax.ShapeDtypeStruct((m, d), BF16)],
        compiler_params=_params("parallel"),
    )(x, w, resid, g.reshape(1, d), b.reshape(1, d))


def _row_iota(rows, cols):
    return lax.broadcasted_iota(jnp.int32, (rows, cols), 0).astype(F32)


def _first_max(cur, iota, n):
    m = jnp.max(cur, axis=0, keepdims=True)
    idx = jnp.min(jnp.where(cur == m, iota, float(n)), axis=0, keepdims=True)
    return m, idx


def _route_kernel(x_ref, wh_ref, wl_ref, bias_ref, e_ref, w_ref, cnt_ref):
    x = x_ref[...]
    tm = x.shape[0]
    xh = x.astype(BF16)
    xl = (x - xh.astype(F32)).astype(BF16)
    wh = wh_ref[...]
    logits = lax.dot_general(wh, xh, _NT, preferred_element_type=F32)
    logits += lax.dot_general(wh, xl, _NT, preferred_element_type=F32)
    logits += lax.dot_general(wl_ref[...], xh, _NT, preferred_element_type=F32)
    scores = jax.nn.sigmoid(logits)
    choice = scores + bias_ref[...]
    gsz = N_EXPERTS // N_GROUPS
    sub_iota = _row_iota(gsz, tm)
    group_rows = []
    for g in range(N_GROUPS):
        grp = choice[g * gsz:(g + 1) * gsz]
        m1, first = _first_max(grp, sub_iota, gsz)
        m2 = jnp.max(jnp.where(sub_iota == first, -jnp.inf, grp), axis=0, keepdims=True)
        group_rows.append(m1 + m2)
    cur = jnp.concatenate(group_rows, axis=0)
    g_iota = _row_iota(N_GROUPS, tm)
    sel = jnp.zeros((N_GROUPS, tm), F32)
    for _ in range(TOPK_GROUPS):
        _, gi = _first_max(cur, g_iota, N_GROUPS)
        pick = g_iota == gi
        sel = jnp.where(pick, 1.0, sel)
        cur = jnp.where(pick, -jnp.inf, cur)
    masked = jnp.concatenate(
        [jnp.where(sel[g:g + 1] > 0.5, choice[g * gsz:(g + 1) * gsz], -jnp.inf)
         for g in range(N_GROUPS)], axis=0)
    e_iota = _row_iota(N_EXPERTS, tm)
    cur = masked
    ids, vals = [], []
    for _ in range(TOP_K):
        _, ei = _first_max(cur, e_iota, N_EXPERTS)
        pick = e_iota == ei
        vals.append(jnp.sum(jnp.where(pick, scores, 0.0), axis=0, keepdims=True))
        ids.append(ei)
        cur = jnp.where(pick, -jnp.inf, cur)
    member = jnp.where((masked > -jnp.inf) & (cur == -jnp.inf), 1.0, 0.0).astype(BF16)
    s = jnp.concatenate(vals, axis=0)
    e_ref[...] = jnp.concatenate(ids, axis=0).astype(jnp.int32)
    w_ref[...] = s / (jnp.sum(s, axis=0, keepdims=True) + 1e-20) * ROUTED_SCALE
    cnt_ref[...] = lax.dot_general(jnp.ones((8, tm), BF16), member, _NT, preferred_element_type=F32)


def _route(x, w_router, router_bias, tm):
    n, d = x.shape
    wt = w_router.T
    wh = wt.astype(BF16)
    wl = (wt - wh.astype(F32)).astype(BF16)
    e, w, cnt = pl.pallas_call(
        _route_kernel,
        grid=(n // tm,),
        in_specs=[_rows(tm, d), _const((N_EXPERTS, d)), _const((N_EXPERTS, d)), _const((N_EXPERTS, 1))],
        out_specs=[pl.BlockSpec((TOP_K, tm), lambda i: (0, i)),
                   pl.BlockSpec((TOP_K, tm), lambda i: (0, i)),
                   pl.BlockSpec((None, 8, N_EXPERTS), lambda i: (i, 0, 0))],
        out_shape=[jax.ShapeDtypeStruct((TOP_K, n), jnp.int32),
                   jax.ShapeDtypeStruct((TOP_K, n), F32),
                   jax.ShapeDtypeStruct((n // tm, 8, N_EXPERTS), F32)],
        compiler_params=_params("parallel"),
    )(x, wh, wl, router_bias.reshape(N_EXPERTS, 1).astype(F32))
    return e, w, cnt[:, 0, :]


def _dest_kernel(e_ref, base_ref, d_ref):
    e = e_ref[...].astype(F32)
    tm = e.shape[1]
    e_iota = _row_iota(N_EXPERTS, tm)
    picks = [e_iota == e[k:k + 1] for k in range(TOP_K)]
    member = jnp.zeros((N_EXPERTS, tm), F32)
    for p in picks:
        member = jnp.where(p, 1.0, member)
    r = lax.broadcasted_iota(jnp.int32, (tm, tm), 0)
    c = lax.broadcasted_iota(jnp.int32, (tm, tm), 1)
    upper = jnp.where(r <= c, 1.0, 0.0).astype(BF16)
    prefix = jnp.dot(member.astype(BF16), upper, preferred_element_type=F32)
    val = prefix + (base_ref[...] - 1.0)
    rows = [jnp.sum(jnp.where(p, val, 0.0), axis=0, keepdims=True) for p in picks]
    d_ref[...] = jnp.concatenate(rows, axis=0).astype(jnp.int32)


def _dest_rows(top_e, tile_base, tm):
    k, n = top_e.shape
    return pl.pallas_call(
        _dest_kernel,
        grid=(n // tm,),
        in_specs=[pl.BlockSpec((k, tm), lambda i: (0, i)),
                  pl.BlockSpec((None, N_EXPERTS, 1), lambda i: (i, 0, 0))],
        out_specs=pl.BlockSpec((k, tm), lambda i: (0, i)),
        out_shape=jax.ShapeDtypeStruct((k, n), jnp.int32),
        compiler_params=_params("parallel"),
    )(top_e, tile_base.astype(F32)[:, :, None])


def _flash_kernel(q_ref, k_ref, v_ref, *rest, groups, dq, dv, tk, row_parts):
    o_ref = rest[-1]
    tq = q_ref.shape[0]
    t = k_ref.shape[0]
    rq = tq // row_parts
    chains = [(g, r * rq) for g in range(groups) for r in range(row_parts)]
    qs = [q_ref[r0:r0 + rq, g * dq:(g + 1) * dq] for g, r0 in chains]

    def body(c, carry):
        start = pl.multiple_of(c * tk, tk)
        k = k_ref[pl.ds(start, tk), :]
        v = v_ref[pl.ds(start, tk), :]
        out = []
        for n in range(len(chains)):
            m, l, acc = carry[n]
            s = lax.dot_general(qs[n], k, _NT, preferred_element_type=F32)
            m_new = jnp.maximum(m, jnp.max(s, axis=-1, keepdims=True))
            alpha = jnp.exp2(m - m_new)
            p = jnp.exp2(s - m_new)
            l = alpha * l + jnp.sum(p, axis=-1, keepdims=True)
            acc = alpha * acc + jnp.dot(p.astype(BF16), v, preferred_element_type=F32)
            out.append((m_new, l, acc))
        return tuple(out)

    init = tuple((jnp.full((rq, 1), NEG_BIG, F32), jnp.zeros((rq, 1), F32),
                  jnp.zeros((rq, dv), F32)) for _ in chains)
    res = lax.fori_loop(0, t // tk, body, init, unroll=4)
    for n, (g, r0) in enumerate(chains):
        _, l, acc = res[n]
        o_ref[r0:r0 + rq, g * dv:(g + 1) * dv] = (acc / l).astype(o_ref.dtype)


def _seq_view(a, t):
    return a.reshape(a.shape[0] // t, t, a.shape[1])


def _carry_output(prev, n_in):
    if prev is None:
        return [], [], {}
    prev = list(prev)
    return prev, [pl.BlockSpec(memory_space=pl.ANY)] * len(prev), {n_in + k: k for k in range(len(prev))}


def _attention(q, k, v, seqs, t, first_seq, prev, *, kv_heads, groups, dq, dv, row_parts, tq=1024):
    n = q.shape[0]
    tq = _tile(t, tq)
    tk = _tile(t, 2048 if t >= 4096 else 1024)
    kern = functools.partial(_flash_kernel, groups=groups, dq=dq, dv=dv, tk=tk, row_parts=row_parts)
    extra, extra_specs, aliases = _carry_output(None if prev is None else [_seq_view(prev, t)], 3)
    out = pl.pallas_call(
        kern,
        grid=(seqs, kv_heads, t // tq),
        in_specs=[pl.BlockSpec((None, tq, groups * dq), lambda i, h, j: (i + first_seq, j, h)),
                  pl.BlockSpec((None, t, dq), lambda i, h, j: (i + first_seq, 0, h)),
                  pl.BlockSpec((None, t, dv), lambda i, h, j: (i + first_seq, 0, h))] + extra_specs,
        out_specs=pl.BlockSpec((None, tq, groups * dv), lambda i, h, j: (i + first_seq, j, h)),
        out_shape=jax.ShapeDtypeStruct((n // t, t, kv_heads * groups * dv), BF16),
        input_output_aliases=aliases,
        compiler_params=_params("parallel", "parallel", "arbitrary"),
    )(_seq_view(q, t), _seq_view(k, t), _seq_view(v, t), *extra)
    return out.reshape(n, -1)


def _cumsum_time(g, rev):
    c = g.shape[0]
    row = lax.broadcasted_iota(jnp.int32, (c, c), 0)
    col = lax.broadcasted_iota(jnp.int32, (c, c), 1)
    tri = jnp.where((col >= row) if rev else (col <= row), 1.0, 0.0).astype(BF16)
    g1 = g.astype(BF16)
    r1 = g - g1.astype(F32)
    g2 = r1.astype(BF16)
    g3 = (r1 - g2.astype(F32)).astype(BF16)
    out = jnp.dot(tri, g1, preferred_element_type=F32)
    out += jnp.dot(tri, g2, preferred_element_type=F32)
    out += jnp.dot(tri, g3, preferred_element_type=F32)
    return out


def _gla_gates(fl, lb, rev):
    fg = lb + (1.0 - lb) * jax.nn.sigmoid(fl)
    return 1.0 - fg, _cumsum_time(jnp.log2(fg), rev)


def _gla_state(q, v, kk, b, st, rev):
    b_edge = b[0:1] if rev else b[GLA_CHUNK - 1:GLA_CHUNK]
    o = lax.dot_general((q * jnp.exp2(b)).astype(BF16), st.astype(BF16), _NT,
                        preferred_element_type=F32)
    kdec = (kk * jnp.exp2(b_edge - b)).astype(BF16)
    st_new = st * jnp.exp2(b_edge) + lax.dot_general(
        v.astype(BF16), kdec, (((0,), (0,)), ((), ())), preferred_element_type=F32)
    return o, st_new


def _gla_earlier_blocks(i, q, v, kk, b, rev):
    cn, sub = GLA_CHUNK, GLA_SUB
    lo, hi = i * sub, (i + 1) * sub
    if rev:
        if hi == cn:
            return None
        ref_b, plo, phi = b[hi:hi + 1], hi, cn
    else:
        if lo == 0:
            return None
        ref_b, plo, phi = b[lo - 1:lo], 0, lo
    qd = (q[lo:hi] * jnp.exp2(b[lo:hi] - ref_b)).astype(BF16)
    kd = (kk[plo:phi] * jnp.exp2(ref_b - b[plo:phi])).astype(BF16)
    a = lax.dot_general(qd, kd, _NT, preferred_element_type=F32)
    return jnp.dot(a.astype(BF16), v[plo:phi].astype(BF16), preferred_element_type=F32)


def _gla_diagonal(i, q, v, kk, b, ones, rev):
    sub = GLA_SUB
    lo = i * sub
    bi, qi, ki, vi = b[lo:lo + sub], q[lo:lo + sub], kk[lo:lo + sub], v[lo:lo + sub]
    t_idx = lax.broadcasted_iota(jnp.int32, (SUBLANES, 1), 0)
    terms, plan = [], []
    for s in range(sub):
        for r0 in range(0, sub, SUBLANES):
            if (r0 > s) if rev else (r0 + SUBLANES - 1 < s):
                continue
            d = bi[r0:r0 + SUBLANES] - bi[s:s + 1]
            if r0 <= s < r0 + SUBLANES:
                keep = (t_idx + r0 <= s) if rev else (t_idx + r0 >= s)
                d = jnp.where(keep, d, NEG_BIG)
            terms.append(qi[r0:r0 + SUBLANES] * jnp.exp2(d) * ki[s:s + 1])
            plan.append((s, r0))
    a = jnp.dot(jnp.concatenate(terms, axis=0).astype(BF16), ones, preferred_element_type=F32)
    halves = {}
    for n, (s, r0) in enumerate(plan):
        term = a[n * SUBLANES:(n + 1) * SUBLANES] * vi[s:s + 1]
        halves[r0] = halves[r0] + term if r0 in halves else term
    return halves


def _hgrn_kernel(qf_ref, ff_ref, vf_ref, qb_ref, fb_ref, vb_ref, lb_ref, *rest):
    of_ref, ob_ref, st_ref = rest[-3:]

    @pl.when(pl.program_id(1) == 0)
    def _():
        st_ref[...] = jnp.zeros_like(st_ref)

    ones = jnp.ones((A_DIM, LANES), BF16)
    dirs = ((qf_ref, ff_ref, vf_ref, of_ref, False), (qb_ref, fb_ref, vb_ref, ob_ref, True))
    work = []
    for di, (q_ref, f_ref, v_ref, o_ref, rev) in enumerate(dirs):
        for h in range(A_HEADS):
            sl = slice(h * A_DIM, (h + 1) * A_DIM)
            kk, b = _gla_gates(f_ref[:, sl], lb_ref[di:di + 1, sl], rev)
            work.append(dict(di=di, h=h, sl=sl, rev=rev, o_ref=o_ref, q=q_ref[:, sl], v=v_ref[:, sl],
                             kk=kk, b=b))
    for w in work:
        w["o"], st_new = _gla_state(w["q"], w["v"], w["kk"], w["b"], st_ref[w["di"], w["h"]], w["rev"])
        st_ref[w["di"], w["h"]] = st_new
    for i in range(GLA_CHUNK // GLA_SUB):
        lo = i * GLA_SUB
        early = [_gla_earlier_blocks(i, w["q"], w["v"], w["kk"], w["b"], w["rev"]) for w in work]
        diag = [_gla_diagonal(i, w["q"], w["v"], w["kk"], w["b"], ones, w["rev"]) for w in work]
        for w, e, dg in zip(work, early, diag):
            oi = w["o"][lo:lo + GLA_SUB]
            if e is not None:
                oi = oi + e
            for r0, val in dg.items():
                w["o_ref"][lo + r0:lo + r0 + SUBLANES, w["sl"]] = oi[r0:r0 + SUBLANES] + val


def _hgrn_scan(ha, lb, seqs, t, first_seq, prev):
    n = ha.shape[0]
    nc = t // GLA_CHUNK
    blk = (None, GLA_CHUNK, A_WIDTH)
    fwd = lambda col: pl.BlockSpec(blk, lambda i, c: (i + first_seq, c, col))
    bwd = lambda col: pl.BlockSpec(blk, lambda i, c: (i + first_seq, nc - 1 - c, col))
    out = jax.ShapeDtypeStruct((n // t, t, A_WIDTH), F32)
    hv = _seq_view(ha, t)
    extra, extra_specs, aliases = _carry_output(
        None if prev is None else [_seq_view(p, t) for p in prev], 7)
    o_f, o_b = pl.pallas_call(
        _hgrn_kernel,
        grid=(seqs, nc),
        in_specs=[fwd(0), fwd(1), fwd(3), bwd(0), bwd(2), bwd(3),
                  pl.BlockSpec((2, A_WIDTH), lambda i, c: (0, 0))] + extra_specs,
        out_specs=[fwd(0), bwd(0)],
        out_shape=[out, out],
        scratch_shapes=[pltpu.VMEM((2, A_HEADS, A_DIM, A_DIM), F32)],
        input_output_aliases=aliases,
        compiler_params=_params("parallel", "arbitrary"),
    )(hv, hv, hv, hv, hv, hv, lb, *extra)
    return o_f.reshape(n, A_WIDTH), o_b.reshape(n, A_WIDTH)


def _swiglu_body(x, wg, wu, wd):
    g = jnp.dot(x, wg, preferred_element_type=F32)
    u = jnp.dot(x, wu, preferred_element_type=F32)
    hid = (g * jax.nn.sigmoid(g) * u).astype(BF16)
    return jnp.dot(hid, wd, preferred_element_type=F32)


def _combine_kernel(y_ref, w_ref, x_ref, xb_ref, wg_ref, wu_ref, wd_ref, g_ref, b_ref, *rest):
    o_ref, ob_ref = rest[-2:]
    w = w_ref[...]
    f = _swiglu_body(xb_ref[...], wg_ref[...], wu_ref[...], wd_ref[...])
    for k in range(TOP_K):
        f = f + y_ref[k].astype(F32) * w[:, k:k + 1]
    out = _layer_norm(ALPHA * x_ref[...] + f, g_ref[...], b_ref[...])
    o_ref[...] = out
    ob_ref[...] = out.astype(BF16)


def _moe_combine(y_tok, w_tok, x, xb, wg, wu, wd, g, b, first_row, prev, tm=256):
    n, d = x.shape
    n_part = y_tok.shape[1]
    f = wg.shape[1]
    tm = _tile(n_part, tm)
    assert first_row % tm == 0
    first = first_row // tm
    rows = lambda width: pl.BlockSpec((tm, width), lambda i: (i + first, 0))
    extra, extra_specs, aliases = _carry_output(prev, 9)
    return pl.pallas_call(
        _combine_kernel,
        grid=(n_part // tm,),
        in_specs=[pl.BlockSpec((TOP_K, tm, d), lambda i: (0, i, 0)), rows(TOP_K), rows(d), rows(d),
                  _const((d, f)), _const((d, f)), _const((f, d)), _const((1, d)), _const((1, d))]
        + extra_specs,
        out_specs=[rows(d), rows(d)],
        out_shape=[jax.ShapeDtypeStruct((n, d), F32), jax.ShapeDtypeStruct((n, d), BF16)],
        input_output_aliases=aliases,
        compiler_params=_params("parallel"),
    )(y_tok, w_tok, x, xb, wg, wu, wd, g.reshape(1, d), b.reshape(1, d), *extra)


def _grouped_kernel(blk_e_ref, nused_ref, x_ref, wg_ref, wu_ref, wd_ref, *rest, first_blk):
    o_ref, wg16, wu16, wd16 = rest[-4:]
    i = pl.program_id(0)
    blk = i + first_blk
    active = blk < nused_ref[0]
    new_expert = (i == 0) | (blk_e_ref[blk] != blk_e_ref[jnp.maximum(blk - 1, 0)])

    @pl.when(active & new_expert)
    def _():
        wg16[...] = wg_ref[0].astype(BF16)
        wu16[...] = wu_ref[0].astype(BF16)
        wd16[...] = wd_ref[0].astype(BF16)

    @pl.when(active)
    def _():
        o_ref[...] = _swiglu_body(x_ref[...], wg16[...], wu16[...], wd16[...]).astype(o_ref.dtype)


def _grouped_experts(x_part, blk_e, nused, layer, wg, wu, wd, first_blk, total_rows, prev):
    p, d = x_part.shape
    f = wg.shape[3]
    nb = p // MOE_ROWS
    active = lambda i, nu: jnp.minimum(i, jnp.clip(nu[0] - first_blk - 1, 0, nb - 1))
    row_in = lambda i, be, nu: (active(i, nu), 0)
    row_out = lambda i, be, nu: (active(i, nu) + first_blk, 0)
    exp = lambda i, be, nu: (layer, be[active(i, nu) + first_blk], 0, 0)
    extra, extra_specs, aliases = _carry_output(None if prev is None else [prev], 6)
    return pl.pallas_call(
        functools.partial(_grouped_kernel, first_blk=first_blk),
        grid_spec=pltpu.PrefetchScalarGridSpec(
            num_scalar_prefetch=2,
            grid=(nb,),
            in_specs=[pl.BlockSpec((MOE_ROWS, d), row_in),
                      pl.BlockSpec((None, 1, d, f), exp),
                      pl.BlockSpec((None, 1, d, f), exp),
                      pl.BlockSpec((None, 1, f, d), exp)] + extra_specs,
            out_specs=pl.BlockSpec((MOE_ROWS, d), row_out),
            scratch_shapes=[pltpu.VMEM((d, f), BF16), pltpu.VMEM((d, f), BF16), pltpu.VMEM((f, d), BF16)]),
        out_shape=jax.ShapeDtypeStruct((total_rows, d), BF16),
        input_output_aliases=aliases,
        compiler_params=_params("arbitrary"),
    )(blk_e, nused, x_part, wg, wu, wd, *extra)


def _moe_block(x, xb, layer, w_router, router_bias, w_gate, w_up, w_down, ws_gate, ws_up, ws_down,
               ln_g, ln_b):
    n, d = x.shape
    tm = TOKEN_TILE
    top_e, top_w, tile_cnt = _route(x, w_router, router_bias, tm)
    tile_cnt = tile_cnt.astype(jnp.int32)
    counts = jnp.sum(tile_cnt, axis=0)
    padded = (counts + MOE_ROWS - 1) // MOE_ROWS * MOE_ROWS
    pend = jnp.cumsum(padded)
    pstart = pend - padded
    tile_base = pstart[None, :] + jnp.cumsum(tile_cnt, axis=0) - tile_cnt
    dest = _dest_rows(top_e, tile_base, tm)
    nblk = (n * TOP_K) // MOE_ROWS + N_EXPERTS
    blk_e = jnp.minimum(jnp.searchsorted(pend, jnp.arange(nblk) * MOE_ROWS, side='right'),
                        N_EXPERTS - 1).astype(jnp.int32)
    nused = (pend[-1] // MOE_ROWS).astype(jnp.int32).reshape(1)
    tok = jnp.broadcast_to(jnp.arange(n, dtype=jnp.int32)[None, :], (TOP_K, n))
    total_rows = nblk * MOE_ROWS
    pad_tok = jnp.arange(total_rows, dtype=jnp.int32) % n
    filled = jnp.zeros((total_rows,), jnp.int32).at[dest.reshape(-1)].add(
        tok.reshape(-1) + 1, unique_indices=True, mode="promise_in_bounds")
    slot_tok = jnp.where(filled > 0, filled - 1, pad_tok)
    assert nblk % MOE_PARTS == 0 and n % MOE_PARTS == 0
    wg, wu, wd = w_gate, w_up, w_down
    part_blks = nblk // MOE_PARTS
    part_rows = part_blks * MOE_ROWS
    y_sorted = None
    for j in range(MOE_PARTS):
        x_part = xb.at[slot_tok[j * part_rows:(j + 1) * part_rows]].get(mode="promise_in_bounds")
        y_sorted = _grouped_experts(x_part, blk_e, nused, layer, wg, wu, wd, j * part_blks, total_rows,
                                    y_sorted)
    w_tok = top_w.T
    sg, su, sd = ws_gate.astype(BF16), ws_up.astype(BF16), ws_down.astype(BF16)
    part_tok = n // MOE_PARTS
    out = None
    for j in range(MOE_PARTS):
        rows = dest[:, j * part_tok:(j + 1) * part_tok].reshape(-1)
        y_tok = y_sorted.at[rows].get(mode="promise_in_bounds").reshape(TOP_K, part_tok, d)
        out = _moe_combine(y_tok, w_tok, x, xb, sg, su, sd, ln_g, ln_b, j * part_tok, out)
    return out


def _per_group(shapes, fn, init):
    out = init
    tok = 0
    for (b, t) in shapes:
        assert tok % t == 0
        out = fn(b, t, tok // t, out)
        tok += b * t
    return out


def _even_mixer(x, xb, shapes, tables, w_in, lb, hgrn_norm_w, q_norm_w, k_norm_w, w_out, ln_g, ln_b):
    ha, q, k, v = _even_prep(xb, shapes, w_in, q_norm_w, k_norm_w, tables)
    lb2 = lb.reshape(2, A_WIDTH)
    n = x.shape[0]
    o_f, o_b = _per_group(shapes, lambda b, t, s0, prev: _hgrn_scan(ha, lb2, b, t, s0, prev),
                          (jnp.zeros((n, A_WIDTH), F32), jnp.zeros((n, A_WIDTH), F32)))
    att = _per_group(shapes, lambda b, t, s0, prev: _attention(
        q, k, v, b, t, s0, prev, kv_heads=B_KV_HEADS, groups=B_GROUP, dq=B_HEAD_DIM, dv=B_HEAD_DIM,
        row_parts=1), jnp.zeros((n, B_WIDTH), BF16))
    return _even_out(o_f, o_b, ha, att, hgrn_norm_w, w_out, x, ln_g, ln_b)


def _mla_mixer(x, xb, shapes, tables, w_in, q_a_norm_w, w_q_b, kv_a_norm_w, w_kv_b, w_out, ln_g, ln_b):
    q, k, v = _mla_prep(xb, shapes, w_in, q_a_norm_w, w_q_b, kv_a_norm_w, w_kv_b, tables)
    att = _per_group(shapes, lambda b, t, s0, prev: _attention(
        q, k, v, b, t, s0, prev, kv_heads=C_HEADS, groups=1, dq=C_QK_PAD, dv=C_V, row_parts=1),
        jnp.zeros((x.shape[0], C_HEADS * C_V), BF16))
    return _matmul_residual_ln(att, w_out.astype(BF16), x, ln_g, ln_b)


def _trunks(xs, params):
    (w_in_even, lb_logits, hgrn_norm_w, q_norm_w, k_norm_w, w_out_even,
     w_in_odd, q_a_norm_w, w_q_b, kv_a_norm_w, w_kv_b, w_out_odd,
     w_router, router_bias, w_gate, w_up, w_down, ws_gate, ws_up, ws_down,
     ln_mix_g, ln_mix_b, ln_ffn_g, ln_ffn_b) = params
    shapes = [(x.shape[0], x.shape[1]) for x in xs]
    for (b, t) in shapes:
        assert t % TOKEN_TILE == 0 and t % GRID_W == 0
    d = xs[0].shape[-1]
    x = jnp.concatenate([a.reshape(-1, d) for a in xs], axis=0)
    xb = x.astype(BF16)
    t_max = max(t for _, t in shapes)
    tables_b = _rope_cs(t_max, B_HEAD_DIM)
    tables_c = _rope_cs(t_max, C_ROPE)
    lb_all = jnp.cumsum(jax.nn.softmax(lb_logits.astype(F32), axis=0), axis=0)
    for l in range(DEPTH):
        j = l // 2
        if l % 2 == 0:
            x, xb = _even_mixer(x, xb, shapes, tables_b, w_in_even[j], lb_all[j], hgrn_norm_w[j],
                                q_norm_w[j], k_norm_w[j], w_out_even[j], ln_mix_g[l], ln_mix_b[l])
        else:
            x, xb = _mla_mixer(x, xb, shapes, tables_c, w_in_odd[j], q_a_norm_w[j], w_q_b[j],
                               kv_a_norm_w[j], w_kv_b[j], w_out_odd[j], ln_mix_g[l], ln_mix_b[l])
        x, xb = _moe_block(x, xb, l, w_router[l], router_bias[l], w_gate, w_up, w_down,
                           ws_gate[l], ws_up[l], ws_down[l], ln_ffn_g[l], ln_ffn_b[l])
    outs = []
    off = 0
    for (b, t) in shapes:
        outs.append(x[off:off + b * t].reshape(b, t, d))
        off += b * t
    return tuple(outs)


def kernel(x_prompt, x_sample, w_in_even, lb_logits, hgrn_norm_w, q_norm_w, k_norm_w, w_out_even,
           w_in_odd, q_a_norm_w, w_q_b, kv_a_norm_w, w_kv_b, w_out_odd,
           w_router, router_bias, w_gate, w_up, w_down, ws_gate, ws_up, ws_down,
           ln_mix_g, ln_mix_b, ln_ffn_g, ln_ffn_b):
    params = (w_in_even, lb_logits, hgrn_norm_w, q_norm_w, k_norm_w, w_out_even,
              w_in_odd, q_a_norm_w, w_q_b, kv_a_norm_w, w_kv_b, w_out_odd,
              w_router, router_bias, w_gate, w_up, w_down, ws_gate, ws_up, ws_down,
              ln_mix_g, ln_mix_b, ln_ffn_g, ln_ffn_b)
    return _trunks([x_prompt, x_sample], params)
```
